```python
import jax, jax.numpy as jnp
from jax import lax
import numpy as np

D_MODEL = 1024
BATCH = 2
SEQ = 8192
DEPTH = 2

HGRN_WIDTH = D_MODEL
HGRN_HEADS = 8
HGRN_EXPAND = HGRN_WIDTH // HGRN_HEADS
HGRN_HEAD_V = HGRN_WIDTH // HGRN_HEADS
HGRN_FDIM = HGRN_HEADS * HGRN_EXPAND
GLA_HEADS = 4
GLA_KEY = D_MODEL // 2
GLA_VAL = D_MODEL
GLA_HEAD_K = GLA_KEY // GLA_HEADS
GLA_HEAD_V = GLA_VAL // GLA_HEADS
GLA_GATE_RANK = 16
GLA_GATE_TAU = 16.0
N_BRANCH = 2
CHUNK = 64
EPS = 1e-6
F_FLOOR = 1e-20
IN_SPLITS = (HGRN_FDIM, HGRN_FDIM, HGRN_WIDTH, HGRN_WIDTH,
             GLA_KEY, GLA_KEY, GLA_VAL, GLA_VAL,
             GLA_GATE_RANK,
             N_BRANCH * D_MODEL)
D_IN = sum(IN_SPLITS)
IN_OFFSETS = tuple(int(v) for v in np.cumsum(IN_SPLITS)[:-1])

kernel_name = "hybrid_hgrn2_gla_adaln"


def rms_norm(x, w):
    xf = x.astype(jnp.float32)
    y = xf * lax.rsqrt(jnp.mean(xf * xf, axis=-1, keepdims=True) + EPS)
    return (y * w.astype(jnp.float32)).astype(x.dtype)


def chunked_gated_linear_attention(q, k, v, log_a, scale):
    B, T, H, K = q.shape
    V = v.shape[-1]
    n = T // CHUNK
    f32 = jnp.float32

    def to_chunks(t):
        return t.astype(f32).reshape(B, n, CHUNK, H, t.shape[-1]).transpose(1, 0, 3, 2, 4)

    qc, kc, vc, gc = to_chunks(q * scale), to_chunks(k), to_chunks(v), to_chunks(log_a)
    causal = jnp.tril(jnp.ones((CHUNK, CHUNK), dtype=bool))[:, :, None]

    def step(S, inp):
        qi, ki, vi, gi = inp
        b = jnp.cumsum(gi, axis=2)
        o_inter = jnp.einsum('bhck,bhkv->bhcv', qi * jnp.exp(b), S)
        diff = b[:, :, :, None, :] - b[:, :, None, :, :]
        decay = jnp.where(causal, jnp.exp(jnp.minimum(diff, 0.0)), 0.0)
        scores = jnp.einsum('bhtk,bhsk,bhtsk->bhts', qi, ki, decay)
        o_intra = jnp.einsum('bhts,bhsv->bhtv', scores, vi)
        b_last = b[:, :, -1:, :]
        S_new = S * jnp.exp(b_last[:, :, 0, :])[..., None] + jnp.einsum(
            'bhsk,bhsv->bhkv', ki * jnp.exp(b_last - b), vi)
        return S_new, o_inter + o_intra

    S0 = jnp.zeros((B, H, K, V), f32)
    _, o = lax.scan(step, S0, (qc, kc, vc, gc))
    return o.transpose(1, 0, 3, 2, 4).reshape(B, T, H, V).astype(v.dtype)


def hgrn2_branch(hq, hf, hi, hz, lb, norm_w):
    B, T, _ = hq.shape
    q = jax.nn.silu(hq).reshape(B, T, HGRN_HEADS, HGRN_EXPAND)
    fr = hf.astype(jnp.float32).reshape(B, T, HGRN_HEADS, HGRN_EXPAND)
    lb = jnp.clip(lb.astype(jnp.float32), 0.0, 1.0).reshape(HGRN_HEADS, HGRN_EXPAND)
    f = lb + (1.0 - lb) * jax.nn.sigmoid(fr)
    log_f = jnp.log(jnp.maximum(f, F_FLOOR))
    k = (1.0 - lb) * jax.nn.sigmoid(-fr)
    v = hi.reshape(B, T, HGRN_HEADS, HGRN_HEAD_V)
    o = chunked_gated_linear_attention(q, k, v, log_f, 1.0)
    o = rms_norm(o.reshape(B, T, HGRN_WIDTH), norm_w)
    return o * jax.nn.silu(hz)


def gla_branch(gq, gk, gv, gz, ga, alpha_w, alpha_b, norm_w):
    B, T, _ = gq.shape
    q = gq.reshape(B, T, GLA_HEADS, GLA_HEAD_K)
    k = gk.reshape(B, T, GLA_HEADS, GLA_HEAD_K)
    v = gv.reshape(B, T, GLA_HEADS, GLA_HEAD_V)
    log_a = jax.nn.log_sigmoid((ga @ alpha_w + alpha_b).astype(jnp.float32)) / GLA_GATE_TAU
    log_a = log_a.reshape(B, T, GLA_HEADS, GLA_HEAD_K)
    o = chunked_gated_linear_attention(q, k, v, log_a, GLA_HEAD_K ** -0.5)
    o = rms_norm(o, norm_w)
    return o.reshape(B, T, GLA_VAL) * jax.nn.silu(gz)


def setup_inputs(seed: int = 0) -> dict:
    key = jax.random.key(seed)
    ks = jax.random.split(key, 16)
    f32 = jnp.float32
    nrm = lambda k, shape, s: jax.random.normal(k, shape, f32) * s
    return {
        "x": nrm(ks[0], (BATCH, SEQ, D_MODEL), 1.0),
        "c": nrm(ks[1], (BATCH, D_MODEL), 1.0),
        "ada_w": nrm(ks[2], (DEPTH, D_MODEL, 3 * D_MODEL), 0.5 * D_MODEL ** -0.5),
        "ada_b": nrm(ks[3], (DEPTH, 3 * D_MODEL), 0.01),
        "norm_w": 1.0 + nrm(ks[4], (DEPTH, D_MODEL), 0.02),
        "w_in": nrm(ks[5], (DEPTH, D_MODEL, D_IN), D_MODEL ** -0.5),
        "hgrn_lb_logits": nrm(ks[6], (DEPTH, HGRN_FDIM), 0.5),
        "hgrn_norm_w": 1.0 + nrm(ks[7], (DEPTH, HGRN_WIDTH), 0.02),
        "gla_alpha_w": nrm(ks[8], (DEPTH, GLA_GATE_RANK, GLA_KEY), GLA_GATE_RANK ** -0.5),
        "gla_alpha_b": nrm(ks[9], (DEPTH, GLA_KEY), 0.1),
        "gla_norm_w": 1.0 + nrm(ks[10], (DEPTH, GLA_HEAD_V), 0.02),
        "w_branch": nrm(ks[11], (DEPTH, N_BRANCH, HGRN_WIDTH, D_MODEL), HGRN_WIDTH ** -0.5),
        "w_out": nrm(ks[12], (DEPTH, D_MODEL, D_MODEL), D_MODEL ** -0.5),
        "final_norm_w": 1.0 + nrm(ks[13], (D_MODEL,), 0.02),
    }


def reference(x, c, ada_w, ada_b, norm_w, w_in, hgrn_lb_logits, hgrn_norm_w, gla_alpha_w,
              gla_alpha_b, gla_norm_w, w_branch, w_out, final_norm_w):
    B, T, _ = x.shape
    c_act = jax.nn.silu(c)
    p = jax.nn.softmax(hgrn_lb_logits.astype(jnp.float32), axis=0)
    lb_all = jnp.cumsum(p, axis=0) - p[0:1]
    for l in range(DEPTH):
        mod = c_act @ ada_w[l] + ada_b[l]
        shift, scale, gate = jnp.split(mod, 3, axis=-1)
        h = rms_norm(x, norm_w[l]) * (1.0 + scale[:, None, :]) + shift[:, None, :]
        proj = h @ w_in[l]
        hq, hf, hi, hz, gq, gk, gv, gz, ga, mg = jnp.split(proj, IN_OFFSETS, axis=-1)
        y_h = hgrn2_branch(hq, hf, hi, hz, lb_all[l], hgrn_norm_w[l])
        y_g = gla_branch(gq, gk, gv, gz, ga, gla_alpha_w[l], gla_alpha_b[l], gla_norm_w[l])
        u_h = y_h @ w_branch[l, 0]
        u_g = y_g @ w_branch[l, 1]
        mg = jax.nn.sigmoid(mg.reshape(B, T, N_BRANCH, D_MODEL))
        merged = mg[:, :, 0] * u_h + mg[:, :, 1] * u_g
        x = x + gate[:, None, :] * (merged @ w_out[l])
    return rms_norm(x, final_norm_w)
```

```python
import functools

import jax
import jax.numpy as jnp
from jax import lax
from jax.experimental import pallas as pl
from jax.experimental.pallas import tpu as pltpu

F32 = jnp.float32
BF16 = jnp.bfloat16

D_MODEL = 1024
DEPTH = 2
HGRN_HEADS = 8
GLA_HEADS = 4
HEAD_K = 128
HGRN_HEAD_V = 128
GLA_HEAD_V = 256
GLA_KEY = GLA_HEADS * HEAD_K
GLA_GATE_RANK = 16
GLA_GATE_TAU = 16.0
EPS = 1e-6
F_FLOOR = 1e-20

LANES = 128
SUBLANES = 8
CHUNK = 64
NBLK = CHUNK // SUBLANES
OFFDIAG_LEVELS = (8, 16, 32)

COL_GA = 7168
N_MAIN = 9216
PROJ_BLOCK = 1024
VMEM_LIMIT = 48 * 1024 * 1024


def _silu(x):
    return x * jax.nn.sigmoid(x)


def _mod_kernel(c_ref, w_ref, b_ref, mod_ref):
    c = c_ref[...]
    c_act = _silu(c)
    mod_ref[0] = jnp.dot(c_act, w_ref[0], preferred_element_type=F32) + b_ref[0]


def _lb_kernel(logit_ref, lb_ref):
    x = logit_ref[...]
    m = jnp.max(x, axis=0, keepdims=True)
    e = jnp.exp(x - m)
    p = e / jnp.sum(e, axis=0, keepdims=True)
    run = jnp.zeros_like(p[0:1])
    for l in range(x.shape[0]):
        run = run + p[l:l + 1]
        lb_ref[l:l + 1, :] = run - p[0:1]


def _prep(c, ada_w, ada_b, lb_logits):
    depth, d, d3 = ada_w.shape
    b = c.shape[0]
    nblk = d3 // d
    mod = pl.pallas_call(
        _mod_kernel,
        grid=(depth, nblk),
        in_specs=[
            pl.BlockSpec((b, d), lambda l, n: (0, 0)),
            pl.BlockSpec((1, d, d), lambda l, n: (l, 0, n)),
            pl.BlockSpec((1, b, d), lambda l, n: (l, 0, n)),
        ],
        out_specs=pl.BlockSpec((1, b, d), lambda l, n: (l, 0, n)),
        out_shape=jax.ShapeDtypeStruct((depth, b, d3), F32),
        name="adaln_mod",
    )(c, ada_w, jnp.broadcast_to(ada_b[:, None, :], (depth, b, d3)))
    lb = pl.pallas_call(
        _lb_kernel,
        out_shape=jax.ShapeDtypeStruct(lb_logits.shape, F32),
        name="hgrn_lower_bounds",
    )(lb_logits)
    return mod, lb


def _inproj_kernel(x_ref, mod_ref, nw_ref, w_ref, wga_ref, aw_ref, ab_ref,
                   out_ref, loga_ref, h_ref):
    @pl.when(pl.program_id(1) == 0)
    def _():
        x = x_ref[...]
        ms = jnp.mean(x * x, axis=-1, keepdims=True)
        y = x * lax.rsqrt(ms + EPS) * nw_ref[...]
        shift = mod_ref[0, 0:1, :]
        scale = mod_ref[0, 1:2, :]
        hb = (y * (1.0 + scale) + shift).astype(BF16)
        h_ref[...] = hb
        ga = jnp.dot(hb, wga_ref[...], preferred_element_type=F32)
        z = jnp.dot(ga.astype(BF16), aw_ref[...], preferred_element_type=F32) + ab_ref[...]
        log_sig = jnp.minimum(z, 0.0) - jnp.log1p(jnp.exp(-jnp.abs(z)))
        loga_ref[...] = log_sig * (1.0 / GLA_GATE_TAU)

    out_ref[...] = jnp.dot(h_ref[...], w_ref[...], preferred_element_type=F32)


def _inproj(xf, mod3, nw, w_main, wga, aw, ab, *, seq, tm=1024, tn=1024):
    m, d = xf.shape
    n = w_main.shape[1]
    per_batch = seq // tm
    return pl.pallas_call(
        _inproj_kernel,
        grid=(m // tm, n // tn),
        in_specs=[
            pl.BlockSpec((tm, d), lambda i, j: (i, 0)),
            pl.BlockSpec((1, 3, d), lambda i, j: (i // per_batch, 0, 0)),
            pl.BlockSpec((1, d), lambda i, j: (0, 0)),
            pl.BlockSpec((d, tn), lambda i, j: (0, j)),
            pl.BlockSpec((d, LANES), lambda i, j: (0, 0)),
            pl.BlockSpec((LANES, GLA_KEY), lambda i, j: (0, 0)),
            pl.BlockSpec((1, GLA_KEY), lambda i, j: (0, 0)),
        ],
        out_specs=[
            pl.BlockSpec((tm, tn), lambda i, j: (i, j)),
            pl.BlockSpec((tm, GLA_KEY), lambda i, j: (i, 0)),
        ],
        out_shape=[
            jax.ShapeDtypeStruct((m, n), F32),
            jax.ShapeDtypeStruct((m, GLA_KEY), F32),
        ],
        scratch_shapes=[pltpu.VMEM((tm, d), BF16)],
        compiler_params=pltpu.CompilerParams(
            dimension_semantics=("parallel", "arbitrary"),
            vmem_limit_bytes=VMEM_LIMIT),
        name="inproj",
    )(xf, mod3, nw, w_main, wga, aw, ab)


def _chunk_masks():
    row = lax.broadcasted_iota(jnp.int32, (CHUNK, CHUNK), 0)
    col = lax.broadcasted_iota(jnp.int32, (CHUNK, CHUNK), 1)
    row_in = row & (SUBLANES - 1)
    row_base = row - row_in
    diag = [(col == row_base + s) & (row_in >= s) for s in range(SUBLANES)]
    levels = []
    for m in OFFDIAG_LEVELS:
        sh = m.bit_length() - 1
        tb = row >> sh
        sb = col >> sh
        levels.append(((tb & 1) == 1) & (sb == tb - 1))
    return diag, levels


def _gla_chunk(q, k, v, g, st, masks):
    diag_masks, level_masks = masks
    kdim = q.shape[-1]
    g3 = g.reshape(NBLK, SUBLANES, kdim)
    sub = lax.broadcasted_iota(jnp.int32, g3.shape, 1)
    c8a = g3
    for sh in (1, 2, 4):
        c8a = c8a + jnp.where(sub >= sh, pltpu.roll(c8a, sh, axis=1), 0.0)

    c = {8: [c8a[i] for i in range(NBLK)]}
    t = {8: [ci[SUBLANES - 1:SUBLANES, :] for ci in c[8]]}
    m = 8
    while m < CHUNK:
        per = m // SUBLANES
        nxt = []
        for i in range(NBLK):
            mb = i // per
            nxt.append(c[m][i] + t[m][mb - 1] if mb % 2 == 1 else c[m][i])
        c[2 * m] = nxt
        t[2 * m] = [t[m][2 * j] + t[m][2 * j + 1] for j in range(len(t[m]) // 2)]
        m *= 2

    def cat(blocks):
        return jnp.concatenate(blocks, axis=0)

    def q_scaled(m):
        return (q * jnp.exp(cat(c[m]))).astype(BF16)

    def k_scaled(m):
        per = m // SUBLANES
        rest = [t[m][i // per] - c[m][i] for i in range(NBLK)]
        return (k * jnp.exp(cat(rest))).astype(BF16)

    nt = (((1,), (1,)), ((), ()))
    q3 = q.reshape(NBLK, SUBLANES, kdim)
    k3 = k.reshape(NBLK, SUBLANES, kdim)
    scores = jnp.zeros((CHUNK, CHUNK), F32)
    for s in range(SUBLANES):
        e = jnp.exp(jnp.minimum(c8a - c8a[:, s:s + 1, :], 0.0))
        p = q3 * e * k3[:, s:s + 1, :]
        r = jnp.sum(p, axis=-1, keepdims=True).reshape(CHUNK, 1)
        scores = scores + jnp.where(diag_masks[s], r, 0.0)
    for m, mask in zip(OFFDIAG_LEVELS, level_masks):
        sc = lax.dot_general(q_scaled(m), k_scaled(m), nt, preferred_element_type=F32)
        scores = scores + jnp.where(mask, sc, 0.0)

    vb = v.astype(BF16)
    o = jnp.dot(scores.astype(BF16), vb, preferred_element_type=F32)
    o = o + lax.dot_general(q_scaled(CHUNK), st.astype(BF16), nt, preferred_element_type=F32)
    upd = lax.dot_general(vb, k_scaled(CHUNK), (((0,), (0,)), ((), ())),
                          preferred_element_type=F32)
    st_new = st * jnp.exp(t[CHUNK][0]) + upd
    return o, st_new


def _hgrn_scan_kernel(hq_ref, hf_ref, hi_ref, lb_ref, o_ref, st_ref, *, tb):
    @pl.when(pl.program_id(1) == 0)
    def _():
        st_ref[...] = jnp.zeros_like(st_ref)

    masks = _chunk_masks()
    lb = jnp.clip(lb_ref[...], 0.0, 1.0)

    def body(ci, carry):
        r0 = pl.multiple_of(ci * CHUNK, CHUNK)
        rows = pl.ds(r0, CHUNK)
        for h in range(HGRN_HEADS):
            cols = slice(h * HEAD_K, (h + 1) * HEAD_K)
            hq = hq_ref[rows, cols]
            lbh = lb[:, cols]
            f = lbh + (1.0 - lbh) * jax.nn.sigmoid(hf_ref[rows, cols])
            g = jnp.log(jnp.maximum(f, F_FLOOR))
            o, st_new = _gla_chunk(_silu(hq), 1.0 - f, hi_ref[rows, cols], g, st_ref[h], masks)
            st_ref[h] = st_new
            o_ref[rows, cols] = o
        return carry

    lax.fori_loop(0, tb // CHUNK, body, 0)


def _gla_scan_kernel(qk_ref, gv_ref, loga_ref, o_ref, st_ref, *, tb):
    @pl.when(pl.program_id(1) == 0)
    def _():
        st_ref[...] = jnp.zeros_like(st_ref)

    masks = _chunk_masks()
    scale = HEAD_K ** -0.5

    def body(ci, carry):
        r0 = pl.multiple_of(ci * CHUNK, CHUNK)
        rows = pl.ds(r0, CHUNK)
        for h in range(GLA_HEADS):
            kc = slice(h * HEAD_K, (h + 1) * HEAD_K)
            kc2 = slice(GLA_KEY + h * HEAD_K, GLA_KEY + (h + 1) * HEAD_K)
            vc = slice(h * GLA_HEAD_V, (h + 1) * GLA_HEAD_V)
            o, st_new = _gla_chunk(qk_ref[rows, kc] * scale, qk_ref[rows, kc2], gv_ref[rows, vc],
                                   loga_ref[rows, kc], st_ref[h], masks)
            st_ref[h] = st_new
            o_ref[rows, vc] = o
        return carry

    lax.fori_loop(0, tb // CHUNK, body, 0)


def _scan_params():
    return pltpu.CompilerParams(dimension_semantics=("parallel", "arbitrary"),
                                vmem_limit_bytes=VMEM_LIMIT)


def _hgrn_scan(proj, lb, *, batch, seq, tb=256):
    m = proj.shape[0]
    nt = seq // tb
    d = D_MODEL
    row = lambda b, t: b * nt + t
    return pl.pallas_call(
        functools.partial(_hgrn_scan_kernel, tb=tb),
        grid=(batch, nt),
        in_specs=[
            pl.BlockSpec((tb, PROJ_BLOCK), lambda b, t: (row(b, t), 0)),
            pl.BlockSpec((tb, PROJ_BLOCK), lambda b, t: (row(b, t), 1)),
            pl.BlockSpec((tb, PROJ_BLOCK), lambda b, t: (row(b, t), 2)),
            pl.BlockSpec((1, d), lambda b, t: (0, 0)),
        ],
        out_specs=pl.BlockSpec((tb, d), lambda b, t: (row(b, t), 0)),
        out_shape=jax.ShapeDtypeStruct((m, d), F32),
        scratch_shapes=[pltpu.VMEM((HGRN_HEADS, HGRN_HEAD_V, HEAD_K), F32)],
        compiler_params=_scan_params(),
        name="hgrn_scan",
    )(proj, proj, proj, lb)


def _gla_scan(proj, loga, *, batch, seq, tb=256):
    m = proj.shape[0]
    nt = seq // tb
    d = D_MODEL
    row = lambda b, t: b * nt + t
    return pl.pallas_call(
        functools.partial(_gla_scan_kernel, tb=tb),
        grid=(batch, nt),
        in_specs=[
            pl.BlockSpec((tb, PROJ_BLOCK), lambda b, t: (row(b, t), 4)),
            pl.BlockSpec((tb, PROJ_BLOCK), lambda b, t: (row(b, t), 5)),
            pl.BlockSpec((tb, GLA_KEY), lambda b, t: (row(b, t), 0)),
        ],
        out_specs=pl.BlockSpec((tb, d), lambda b, t: (row(b, t), 0)),
        out_shape=jax.ShapeDtypeStruct((m, d), F32),
        scratch_shapes=[pltpu.VMEM((GLA_HEADS, GLA_HEAD_V, HEAD_K), F32)],
        compiler_params=_scan_params(),
        name="gla_scan",
    )(proj, proj, loga)


def _rms(x, w):
    return x * lax.rsqrt(jnp.mean(x * x, axis=-1, keepdims=True) + EPS) * w


def _out_kernel(oh_ref, og_ref, hz_ref, gz_ref, mg0_ref, mg1_ref, x_ref, mod_ref,
                hnw_ref, gnw_ref, wb0_ref, wb1_ref, wo_ref, fnw_ref, out_ref, *, final):
    yh = _rms(oh_ref[...], hnw_ref[...]) * _silu(hz_ref[...])
    gnw = gnw_ref[...]
    og = jnp.concatenate(
        [_rms(og_ref[:, h * GLA_HEAD_V:(h + 1) * GLA_HEAD_V], gnw) for h in range(GLA_HEADS)],
        axis=-1)
    yg = og * _silu(gz_ref[...])
    uh = jnp.dot(yh.astype(BF16), wb0_ref[...], preferred_element_type=F32)
    ug = jnp.dot(yg.astype(BF16), wb1_ref[...], preferred_element_type=F32)
    merged = jax.nn.sigmoid(mg0_ref[...]) * uh + jax.nn.sigmoid(mg1_ref[...]) * ug
    gate = mod_ref[0, 2:3, :]
    out = x_ref[...] + gate * jnp.dot(merged.astype(BF16), wo_ref[...],
                                      preferred_element_type=F32)
    if final:
        out = _rms(out, fnw_ref[...])
    out_ref[...] = out


def _outproj(oh, og, proj, xf, mod3, hnw, gnw, wb0, wb1, wo, fnw, *, seq, final, tm=256):
    m, d = xf.shape
    per_batch = seq // tm
    rows = lambda c: pl.BlockSpec((tm, PROJ_BLOCK), lambda i: (i, c))
    const = lambda shape: pl.BlockSpec(shape, lambda i: tuple(0 for _ in shape))
    return pl.pallas_call(
        functools.partial(_out_kernel, final=final),
        grid=(m // tm,),
        in_specs=[
            rows(0), rows(0), rows(3), rows(6), rows(7), rows(8), rows(0),
            pl.BlockSpec((1, 3, d), lambda i: (i // per_batch, 0, 0)),
            const((1, d)), const((1, GLA_HEAD_V)),
            const((d, d)), const((d, d)), const((d, d)), const((1, d)),
        ],
        out_specs=rows(0),
        out_shape=jax.ShapeDtypeStruct((m, d), F32),
        compiler_params=pltpu.CompilerParams(
            dimension_semantics=("parallel",), vmem_limit_bytes=VMEM_LIMIT),
        name="outproj",
    )(oh, og, proj, proj, proj, proj, xf, mod3, hnw, gnw, wb0, wb1, wo, fnw)


def kernel(x, c, ada_w, ada_b, norm_w, w_in, hgrn_lb_logits, hgrn_norm_w, gla_alpha_w,
           gla_alpha_b, gla_norm_w, w_branch, w_out, final_norm_w):
    batch, seq, d = x.shape
    depth = ada_w.shape[0]
    mod_all, lb_all = _prep(c, ada_w, ada_b, hgrn_lb_logits)
    xf = x.reshape(batch * seq, d)
    ga_hi = COL_GA + GLA_GATE_RANK
    for l in range(depth):
        w_main = jnp.concatenate([w_in[l][:, :COL_GA], w_in[l][:, ga_hi:]], axis=1).astype(BF16)
        wga = jnp.pad(w_in[l][:, COL_GA:ga_hi], ((0, 0), (0, LANES - GLA_GATE_RANK))).astype(BF16)
        aw = jnp.pad(gla_alpha_w[l], ((0, LANES - GLA_GATE_RANK), (0, 0))).astype(BF16)
        mod3 = mod_all[l].reshape(batch, 3, d)
        proj, loga = _inproj(xf, mod3, norm_w[l][None], w_main, wga, aw,
                             gla_alpha_b[l][None], seq=seq)
        oh = _hgrn_scan(proj, lb_all[l][None], batch=batch, seq=seq)
        og = _gla_scan(proj, loga, batch=batch, seq=seq)
        xf = _outproj(oh, og, proj, xf, mod3, hgrn_norm_w[l][None], gla_norm_w[l][None],
                      w_branch[l, 0].astype(BF16), w_branch[l, 1].astype(BF16),
                      w_out[l].astype(BF16), final_norm_w[None],
                      seq=seq, final=(l == depth - 1))
    return xf.reshape(batch, seq, d)
```

```python
import functools

import jax
import jax.numpy as jnp
from jax import lax
from jax.experimental import pallas as pl
from jax.experimental.pallas import tpu as pltpu

F32 = jnp.float32
BF16 = jnp.bfloat16

D_MODEL = 1024
DEPTH = 2
HGRN_HEADS = 8
GLA_HEADS = 4
HEAD_K = 128
HGRN_HEAD_V = 128
GLA_HEAD_V = 256
GLA_KEY = GLA_HEADS * HEAD_K
GLA_GATE_RANK = 16
GLA_GATE_TAU = 16.0
EPS = 1e-6
F_FLOOR = 1e-20

LANES = 128
SUBLANES = 8
CHUNK = 64
NBLK = CHUNK // SUBLANES
HALF = CHUNK // 2
SAFE_DECAY = 60.0
OFFDIAG_LEVELS = (8, 16, 32)

COL_GA = 7168
N_MAIN = 9216
PROJ_BLOCK = 1024
VMEM_LIMIT = 48 * 1024 * 1024


def _silu(x):
    return x * jax.nn.sigmoid(x)


def _mod_kernel(c_ref, w_ref, b_ref, mod_ref):
    c = c_ref[...]
    c_act = _silu(c)
    mod_ref[0] = jnp.dot(c_act, w_ref[0], preferred_element_type=F32) + b_ref[0]


def _lb_kernel(logit_ref, lb_ref):
    x = logit_ref[...]
    m = jnp.max(x, axis=0, keepdims=True)
    e = jnp.exp(x - m)
    p = e / jnp.sum(e, axis=0, keepdims=True)
    run = jnp.zeros_like(p[0:1])
    for l in range(x.shape[0]):
        run = run + p[l:l + 1]
        lb_ref[l:l + 1, :] = run - p[0:1]


def _prep(c, ada_w, ada_b, lb_logits):
    depth, d, d3 = ada_w.shape
    b = c.shape[0]
    nblk = d3 // d
    mod = pl.pallas_call(
        _mod_kernel,
        grid=(depth, nblk),
        in_specs=[
            pl.BlockSpec((b, d), lambda l, n: (0, 0)),
            pl.BlockSpec((1, d, d), lambda l, n: (l, 0, n)),
            pl.BlockSpec((1, b, d), lambda l, n: (l, 0, n)),
        ],
        out_specs=pl.BlockSpec((1, b, d), lambda l, n: (l, 0, n)),
        out_shape=jax.ShapeDtypeStruct((depth, b, d3), F32),
        name="adaln_mod",
    )(c, ada_w, jnp.broadcast_to(ada_b[:, None, :], (depth, b, d3)))
    lb = pl.pallas_call(
        _lb_kernel,
        out_shape=jax.ShapeDtypeStruct(lb_logits.shape, F32),
        name="hgrn_lower_bounds",
    )(lb_logits)
    return mod, lb


def _inproj_kernel(x_ref, mod_ref, nw_ref, w_ref, wga_ref, aw_ref, ab_ref,
                   out_ref, loga_ref, h_ref):
    @pl.when(pl.program_id(1) == 0)
    def _():
        x = x_ref[...]
        ms = jnp.mean(x * x, axis=-1, keepdims=True)
        y = x * lax.rsqrt(ms + EPS) * nw_ref[...]
        shift = mod_ref[0, 0:1, :]
        scale = mod_ref[0, 1:2, :]
        hb = (y * (1.0 + scale) + shift).astype(BF16)
        h_ref[...] = hb
        ga = jnp.dot(hb, wga_ref[...], preferred_element_type=F32)
        z = jnp.dot(ga.astype(BF16), aw_ref[...], preferred_element_type=F32) + ab_ref[...]
        log_sig = jnp.minimum(z, 0.0) - jnp.log1p(jnp.exp(-jnp.abs(z)))
        loga_ref[...] = log_sig * (1.0 / GLA_GATE_TAU)

    out_ref[...] = jnp.dot(h_ref[...], w_ref[...], preferred_element_type=F32)


def _inproj(xf, mod3, nw, w_main, wga, aw, ab, *, seq, tm=1024, tn=1024):
    m, d = xf.shape
    n = w_main.shape[1]
    per_batch = seq // tm
    return pl.pallas_call(
        _inproj_kernel,
        grid=(m // tm, n // tn),
        in_specs=[
            pl.BlockSpec((tm, d), lambda i, j: (i, 0)),
            pl.BlockSpec((1, 3, d), lambda i, j: (i // per_batch, 0, 0)),
            pl.BlockSpec((1, d), lambda i, j: (0, 0)),
            pl.BlockSpec((d, tn), lambda i, j: (0, j)),
            pl.BlockSpec((d, LANES), lambda i, j: (0, 0)),
            pl.BlockSpec((LANES, GLA_KEY), lambda i, j: (0, 0)),
            pl.BlockSpec((1, GLA_KEY), lambda i, j: (0, 0)),
        ],
        out_specs=[
            pl.BlockSpec((tm, tn), lambda i, j: (i, j)),
            pl.BlockSpec((tm, GLA_KEY), lambda i, j: (i, 0)),
        ],
        out_shape=[
            jax.ShapeDtypeStruct((m, n), F32),
            jax.ShapeDtypeStruct((m, GLA_KEY), F32),
        ],
        scratch_shapes=[pltpu.VMEM((tm, d), BF16)],
        compiler_params=pltpu.CompilerParams(
            dimension_semantics=("parallel", "arbitrary"),
            vmem_limit_bytes=VMEM_LIMIT),
        name="inproj",
    )(xf, mod3, nw, w_main, wga, aw, ab)


def _fast_masks():
    row = lax.broadcasted_iota(jnp.int32, (HALF, CHUNK), 0)
    col = lax.broadcasted_iota(jnp.int32, (HALF, CHUNK), 1)
    top = col <= row
    left = col < HALF
    bot = (col >= HALF) & (col <= row + HALF)
    return top, left, bot


def _block_cumsum8(g):
    g3 = g.reshape(NBLK, SUBLANES, g.shape[-1])
    sub = lax.broadcasted_iota(jnp.int32, g3.shape, 1)
    for sh in (1, 2, 4):
        g3 = g3 + jnp.where(sub >= sh, pltpu.roll(g3, sh, axis=1), 0.0)
    return g3


_NT = (((1,), (1,)), ((), ()))
_TN = (((0,), (0,)), ((), ()))


def _gla_chunk_fast(q, k, v, g, st, masks):
    top_mask, left_mask, bot_mask = masks
    c8a = _block_cumsum8(g)
    c8 = [c8a[i] for i in range(NBLK)]
    t8 = [ci[SUBLANES - 1:SUBLANES, :] for ci in c8]
    per = HALF // SUBLANES
    c32, t32 = [], []
    for half in range(2):
        run = None
        for i in range(half * per, (half + 1) * per):
            c32.append(c8[i] if run is None else c8[i] + run)
            run = t8[i] if run is None else run + t8[i]
        t32.append(run)
    c32 = jnp.concatenate(c32, axis=0)
    e0 = jnp.exp(t32[0])
    e1 = jnp.exp(t32[1])
    e01 = e0 * e1
    q32 = q * jnp.exp(c32)
    kd = k * jnp.exp(-c32)
    q32b = q32.astype(BF16)
    q64_hi = (q32[HALF:] * e0).astype(BF16)
    lhs = jnp.concatenate([q32b, q64_hi], axis=0)
    sc = lax.dot_general(lhs, kd.astype(BF16), _NT, preferred_element_type=F32)
    top = jnp.where(top_mask, sc[:HALF], 0.0)
    bot = jnp.where(left_mask, sc[CHUNK:], jnp.where(bot_mask, sc[HALF:CHUNK], 0.0))
    scores = jnp.concatenate([top, bot], axis=0).astype(BF16)

    vb = v.astype(BF16)
    q64 = jnp.concatenate([q32b[:HALF], q64_hi], axis=0)
    o = jnp.dot(scores, vb, preferred_element_type=F32)
    o = o + lax.dot_general(q64, st.astype(BF16), _NT, preferred_element_type=F32)
    k64 = jnp.concatenate([kd[:HALF] * e01, kd[HALF:] * e1], axis=0).astype(BF16)
    upd = lax.dot_general(vb, k64, _TN, preferred_element_type=F32)
    return o, st * e01 + upd


def _half_block_decay_ok(g):
    rows, cols = g.shape
    tot = jnp.sum(g.reshape(rows // HALF, HALF, cols), axis=1)
    return jnp.min(tot) >= -SAFE_DECAY


def _exact_masks():
    row = lax.broadcasted_iota(jnp.int32, (CHUNK, CHUNK), 0)
    col = lax.broadcasted_iota(jnp.int32, (CHUNK, CHUNK), 1)
    row_in = row & (SUBLANES - 1)
    row_base = row - row_in
    diag = [(col == row_base + s) & (row_in >= s) for s in range(SUBLANES)]
    levels = []
    for m in OFFDIAG_LEVELS:
        sh = m.bit_length() - 1
        tb = row >> sh
        sb = col >> sh
        levels.append(((tb & 1) == 1) & (sb == tb - 1))
    return diag, levels


def _gla_chunk_exact(q, k, v, g, st, masks):
    diag_masks, level_masks = masks
    kdim = q.shape[-1]
    c8a = _block_cumsum8(g)

    c = {8: [c8a[i] for i in range(NBLK)]}
    t = {8: [ci[SUBLANES - 1:SUBLANES, :] for ci in c[8]]}
    m = 8
    while m < CHUNK:
        per = m // SUBLANES
        nxt = []
        for i in range(NBLK):
            mb = i // per
            nxt.append(c[m][i] + t[m][mb - 1] if mb % 2 == 1 else c[m][i])
        c[2 * m] = nxt
        t[2 * m] = [t[m][2 * j] + t[m][2 * j + 1] for j in range(len(t[m]) // 2)]
        m *= 2

    def cat(blocks):
        return jnp.concatenate(blocks, axis=0)

    def q_scaled(m):
        return (q * jnp.exp(cat(c[m]))).astype(BF16)

    def k_scaled(m):
        per = m // SUBLANES
        rest = [t[m][i // per] - c[m][i] for i in range(NBLK)]
        return (k * jnp.exp(cat(rest))).astype(BF16)

    q3 = q.reshape(NBLK, SUBLANES, kdim)
    k3 = k.reshape(NBLK, SUBLANES, kdim)
    scores = jnp.zeros((CHUNK, CHUNK), F32)
    for s in range(SUBLANES):
        e = jnp.exp(jnp.minimum(c8a - c8a[:, s:s + 1, :], 0.0))
        p = q3 * e * k3[:, s:s + 1, :]
        r = jnp.sum(p, axis=-1, keepdims=True).reshape(CHUNK, 1)
        scores = scores + jnp.where(diag_masks[s], r, 0.0)
    for m, mask in zip(OFFDIAG_LEVELS, level_masks):
        sc = lax.dot_general(q_scaled(m), k_scaled(m), _NT, preferred_element_type=F32)
        scores = scores + jnp.where(mask, sc, 0.0)

    vb = v.astype(BF16)
    o = jnp.dot(scores.astype(BF16), vb, preferred_element_type=F32)
    o = o + lax.dot_general(q_scaled(CHUNK), st.astype(BF16), _NT, preferred_element_type=F32)
    upd = lax.dot_general(vb, k_scaled(CHUNK), _TN, preferred_element_type=F32)
    st_new = st * jnp.exp(t[CHUNK][0]) + upd
    return o, st_new


def _hgrn_scan_kernel(hq_ref, hf_ref, hi_ref, lb_ref, o_ref, st_ref, g_ref, k_ref, *, tb):
    @pl.when(pl.program_id(1) == 0)
    def _():
        st_ref[...] = jnp.zeros_like(st_ref)

    lb = jnp.clip(lb_ref[...], 0.0, 1.0)
    f = lb + (1.0 - lb) * jax.nn.sigmoid(hf_ref[...])
    g_ref[...] = jnp.log(jnp.maximum(f, F_FLOOR))
    k_ref[...] = 1.0 - f
    safe = _half_block_decay_ok(g_ref[...])

    def run(chunk_fn, masks):
        def body(ci, carry):
            r0 = pl.multiple_of(ci * CHUNK, CHUNK)
            rows = pl.ds(r0, CHUNK)
            for h in range(HGRN_HEADS):
                cols = slice(h * HEAD_K, (h + 1) * HEAD_K)
                o, st_new = chunk_fn(_silu(hq_ref[rows, cols]), k_ref[rows, cols],
                                     hi_ref[rows, cols], g_ref[rows, cols], st_ref[h], masks)
                st_ref[h] = st_new
                o_ref[rows, cols] = o
            return carry

        lax.fori_loop(0, tb // CHUNK, body, 0)

    @pl.when(safe)
    def _():
        run(_gla_chunk_fast, _fast_masks())

    @pl.when(jnp.logical_not(safe))
    def _():
        run(_gla_chunk_exact, _exact_masks())


def _gla_scan_kernel(qk_ref, gv_ref, loga_ref, o_ref, st_ref, *, tb):
    @pl.when(pl.program_id(1) == 0)
    def _():
        st_ref[...] = jnp.zeros_like(st_ref)

    scale = HEAD_K ** -0.5
    safe = _half_block_decay_ok(loga_ref[...])

    def run(chunk_fn, masks):
        def body(ci, carry):
            r0 = pl.multiple_of(ci * CHUNK, CHUNK)
            rows = pl.ds(r0, CHUNK)
            for h in range(GLA_HEADS):
                kc = slice(h * HEAD_K, (h + 1) * HEAD_K)
                kc2 = slice(GLA_KEY + h * HEAD_K, GLA_KEY + (h + 1) * HEAD_K)
                vc = slice(h * GLA_HEAD_V, (h + 1) * GLA_HEAD_V)
                o, st_new = chunk_fn(qk_ref[rows, kc] * scale, qk_ref[rows, kc2],
                                     gv_ref[rows, vc], loga_ref[rows, kc], st_ref[h], masks)
                st_ref[h] = st_new
                o_ref[rows, vc] = o
            return carry

        lax.fori_loop(0, tb // CHUNK, body, 0)

    @pl.when(safe)
    def _():
        run(_gla_chunk_fast, _fast_masks())

    @pl.when(jnp.logical_not(safe))
    def _():
        run(_gla_chunk_exact, _exact_masks())


def _scan_params():
    return pltpu.CompilerParams(dimension_semantics=("parallel", "arbitrary"),
                                vmem_limit_bytes=VMEM_LIMIT)


def _hgrn_scan(proj, lb, *, batch, seq, tb=256):
    m = proj.shape[0]
    nt = seq // tb
    d = D_MODEL
    row = lambda b, t: b * nt + t
    return pl.pallas_call(
        functools.partial(_hgrn_scan_kernel, tb=tb),
        grid=(batch, nt),
        in_specs=[
            pl.BlockSpec((tb, PROJ_BLOCK), lambda b, t: (row(b, t), 0)),
            pl.BlockSpec((tb, PROJ_BLOCK), lambda b, t: (row(b, t), 1)),
            pl.BlockSpec((tb, PROJ_BLOCK), lambda b, t: (row(b, t), 2)),
            pl.BlockSpec((1, d), lambda b, t: (0, 0)),
        ],
        out_specs=pl.BlockSpec((tb, d), lambda b, t: (row(b, t), 0)),
        out_shape=jax.ShapeDtypeStruct((m, d), F32),
        scratch_shapes=[pltpu.VMEM((HGRN_HEADS, HGRN_HEAD_V, HEAD_K), F32),
                        pltpu.VMEM((tb, d), F32), pltpu.VMEM((tb, d), F32)],
        compiler_params=_scan_params(),
        name="hgrn_scan",
    )(proj, proj, proj, lb)


def _gla_scan(proj, loga, *, batch, seq, tb=256):
    m = proj.shape[0]
    nt = seq // tb
    d = D_MODEL
    row = lambda b, t: b * nt + t
    return pl.pallas_call(
        functools.partial(_gla_scan_kernel, tb=tb),
        grid=(batch, nt),
        in_specs=[
            pl.BlockSpec((tb, PROJ_BLOCK), lambda b, t: (row(b, t), 4)),
            pl.BlockSpec((tb, PROJ_BLOCK), lambda b, t: (row(b, t), 5)),
            pl.BlockSpec((tb, GLA_KEY), lambda b, t: (row(b, t), 0)),
        ],
        out_specs=pl.BlockSpec((tb, d), lambda b, t: (row(b, t), 0)),
        out_shape=jax.ShapeDtypeStruct((m, d), F32),
        scratch_shapes=[pltpu.VMEM((GLA_HEADS, GLA_HEAD_V, HEAD_K), F32)],
        compiler_params=_scan_params(),
        name="gla_scan",
    )(proj, proj, loga)


def _rms(x, w):
    return x * lax.rsqrt(jnp.mean(x * x, axis=-1, keepdims=True) + EPS) * w


def _out_kernel(oh_ref, og_ref, hz_ref, gz_ref, mg0_ref, mg1_ref, x_ref, mod_ref,
                hnw_ref, gnw_ref, wb0_ref, wb1_ref, wo_ref, fnw_ref, out_ref, *, final):
    yh = _rms(oh_ref[...], hnw_ref[...]) * _silu(hz_ref[...])
    gnw = gnw_ref[...]
    og = jnp.concatenate(
        [_rms(og_ref[:, h * GLA_HEAD_V:(h + 1) * GLA_HEAD_V], gnw) for h in range(GLA_HEADS)],
        axis=-1)
    yg = og * _silu(gz_ref[...])
    uh = jnp.dot(yh.astype(BF16), wb0_ref[...], preferred_element_type=F32)
    ug = jnp.dot(yg.astype(BF16), wb1_ref[...], preferred_element_type=F32)
    merged = jax.nn.sigmoid(mg0_ref[...]) * uh + jax.nn.sigmoid(mg1_ref[...]) * ug
    gate = mod_ref[0, 2:3, :]
    out = x_ref[...] + gate * jnp.dot(merged.astype(BF16), wo_ref[...],
                                      preferred_element_type=F32)
    if final:
        out = _rms(out, fnw_ref[...])
    out_ref[...] = out


def _outproj(oh, og, proj, xf, mod3, hnw, gnw, wb0, wb1, wo, fnw, *, seq, final, tm=256):
    m, d = xf.shape
    per_batch = seq // tm
    rows = lambda c: pl.BlockSpec((tm, PROJ_BLOCK), lambda i: (i, c))
    const = lambda shape: pl.BlockSpec(shape, lambda i: tuple(0 for _ in shape))
    return pl.pallas_call(
        functools.partial(_out_kernel, final=final),
        grid=(m // tm,),
        in_specs=[
            rows(0), rows(0), rows(3), rows(6), rows(7), rows(8), rows(0),
            pl.BlockSpec((1, 3, d), lambda i: (i // per_batch, 0, 0)),
            const((1, d)), const((1, GLA_HEAD_V)),
            const((d, d)), const((d, d)), const((d, d)), const((1, d)),
        ],
        out_specs=rows(0),
        out_shape=jax.ShapeDtypeStruct((m, d), F32),
        compiler_params=pltpu.CompilerParams(
            dimension_semantics=("parallel",), vmem_limit_bytes=VMEM_LIMIT),
        name="outproj",
    )(oh, og, proj, proj, proj, proj, xf, mod3, hnw, gnw, wb0, wb1, wo, fnw)


def kernel(x, c, ada_w, ada_b, norm_w, w_in, hgrn_lb_logits, hgrn_norm_w, gla_alpha_w,
           gla_alpha_b, gla_norm_w, w_branch, w_out, final_norm_w):
    batch, seq, d = x.shape
    depth = ada_w.shape[0]
    mod_all, lb_all = _prep(c, ada_w, ada_b, hgrn_lb_logits)
    xf = x.reshape(batch * seq, d)
    ga_hi = COL_GA + GLA_GATE_RANK
    for l in range(depth):
        w_main = jnp.concatenate([w_in[l][:, :COL_GA], w_in[l][:, ga_hi:]], axis=1).astype(BF16)
        wga = jnp.pad(w_in[l][:, COL_GA:ga_hi], ((0, 0), (0, LANES - GLA_GATE_RANK))).astype(BF16)
        aw = jnp.pad(gla_alpha_w[l], ((0, LANES - GLA_GATE_RANK), (0, 0))).astype(BF16)
        mod3 = mod_all[l].reshape(batch, 3, d)
        proj, loga = _inproj(xf, mod3, norm_w[l][None], w_main, wga, aw,
                             gla_alpha_b[l][None], seq=seq)
        oh = _hgrn_scan(proj, lb_all[l][None], batch=batch, seq=seq)
        og = _gla_scan(proj, loga, batch=batch, seq=seq)
        xf = _outproj(oh, og, proj, xf, mod3, hgrn_norm_w[l][None], gla_norm_w[l][None],
                      w_branch[l, 0].astype(BF16), w_branch[l, 1].astype(BF16),
                      w_out[l].astype(BF16), final_norm_w[None],
                      seq=seq, final=(l == depth - 1))
    return xf.reshape(batch, seq, d)
```

```python
import functools

import jax
import jax.numpy as jnp
from jax import lax
from jax.experimental import pallas as pl
from jax.experimental.pallas import tpu as pltpu

F32 = jnp.float32
BF16 = jnp.bfloat16

D_MODEL = 1024
DEPTH = 2
HGRN_HEADS = 8
GLA_HEADS = 4
HEAD_K = 128
HGRN_HEAD_V = 128
GLA_HEAD_V = 256
GLA_KEY = GLA_HEADS * HEAD_K
GLA_GATE_RANK = 16
GLA_GATE_TAU = 16.0
EPS = 1e-6
F_FLOOR = 1e-20

LANES = 128
SUBLANES = 8
CHUNK = 64
NBLK = CHUNK // SUBLANES
HALF = CHUNK // 2
SAFE_DECAY = 60.0
OFFDIAG_LEVELS = (8, 16, 32)

COL_HF = 1024
COL_HI = 2048
COL_GA = 7168
PROJ_BLOCK = 1024
BLK_HQ, BLK_HI, BLK_HZ, BLK_GQK, BLK_GV, BLK_GZ, BLK_MG0, BLK_MG1 = range(8)
N_BF16_BLOCKS = 8
VMEM_LIMIT = 48 * 1024 * 1024


def _silu(x):
    return x * jax.nn.sigmoid(x)


def _mod_kernel(c_ref, w_ref, b_ref, mod_ref):
    c = c_ref[...]
    c_act = _silu(c)
    mod_ref[0] = jnp.dot(c_act, w_ref[0], preferred_element_type=F32) + b_ref[0]


def _lb_kernel(logit_ref, lb_ref):
    x = logit_ref[...]
    m = jnp.max(x, axis=0, keepdims=True)
    e = jnp.exp(x - m)
    p = e / jnp.sum(e, axis=0, keepdims=True)
    run = jnp.zeros_like(p[0:1])
    for l in range(x.shape[0]):
        run = run + p[l:l + 1]
        lb_ref[l:l + 1, :] = run - p[0:1]


def _prep(c, ada_w, ada_b, lb_logits):
    depth, d, d3 = ada_w.shape
    b = c.shape[0]
    nblk = d3 // d
    mod = pl.pallas_call(
        _mod_kernel,
        grid=(depth, nblk),
        in_specs=[
            pl.BlockSpec((b, d), lambda l, n: (0, 0)),
            pl.BlockSpec((1, d, d), lambda l, n: (l, 0, n)),
            pl.BlockSpec((1, b, d), lambda l, n: (l, 0, n)),
        ],
        out_specs=pl.BlockSpec((1, b, d), lambda l, n: (l, 0, n)),
        out_shape=jax.ShapeDtypeStruct((depth, b, d3), F32),
        name="adaln_mod",
    )(c, ada_w, jnp.broadcast_to(ada_b[:, None, :], (depth, b, d3)))
    lb = pl.pallas_call(
        _lb_kernel,
        out_shape=jax.ShapeDtypeStruct(lb_logits.shape, F32),
        name="hgrn_lower_bounds",
    )(lb_logits)
    return mod, lb


def _inproj_kernel(x_ref, mod_ref, nw_ref, w_ref, whf_ref, wga_ref, aw_ref, ab_ref,
                   out_ref, hf_ref, loga_ref, h_ref):
    @pl.when(pl.program_id(1) == 0)
    def _():
        x = x_ref[...]
        ms = jnp.mean(x * x, axis=-1, keepdims=True)
        y = x * lax.rsqrt(ms + EPS) * nw_ref[...]
        shift = mod_ref[0, 0:1, :]
        scale = mod_ref[0, 1:2, :]
        hb = (y * (1.0 + scale) + shift).astype(BF16)
        h_ref[...] = hb
        hf_ref[...] = jnp.dot(hb, whf_ref[...], preferred_element_type=F32)
        ga = jnp.dot(hb, wga_ref[...], preferred_element_type=F32)
        z = jnp.dot(ga.astype(BF16), aw_ref[...], preferred_element_type=F32) + ab_ref[...]
        log_sig = jnp.minimum(z, 0.0) - jnp.log1p(jnp.exp(-jnp.abs(z)))
        loga_ref[...] = log_sig * (1.0 / GLA_GATE_TAU)

    out_ref[...] = jnp.dot(h_ref[...], w_ref[...], preferred_element_type=F32).astype(BF16)


def _inproj(xf, mod3, nw, w_main, w_hf, wga, aw, ab, *, seq, tm=1024, tn=1024):
    m, d = xf.shape
    n = w_main.shape[1]
    per_batch = seq // tm
    const = lambda shape: pl.BlockSpec(shape, lambda i, j: (0, 0))
    return pl.pallas_call(
        _inproj_kernel,
        grid=(m // tm, n // tn),
        in_specs=[
            pl.BlockSpec((tm, d), lambda i, j: (i, 0)),
            pl.BlockSpec((1, 3, d), lambda i, j: (i // per_batch, 0, 0)),
            const((1, d)),
            pl.BlockSpec((d, tn), lambda i, j: (0, j)),
            const((d, PROJ_BLOCK)),
            const((d, LANES)),
            const((LANES, GLA_KEY)),
            const((1, GLA_KEY)),
        ],
        out_specs=[
            pl.BlockSpec((tm, tn), lambda i, j: (i, j)),
            pl.BlockSpec((tm, PROJ_BLOCK), lambda i, j: (i, 0)),
            pl.BlockSpec((tm, GLA_KEY), lambda i, j: (i, 0)),
        ],
        out_shape=[
            jax.ShapeDtypeStruct((m, n), BF16),
            jax.ShapeDtypeStruct((m, PROJ_BLOCK), F32),
            jax.ShapeDtypeStruct((m, GLA_KEY), F32),
        ],
        scratch_shapes=[pltpu.VMEM((tm, d), BF16)],
        compiler_params=pltpu.CompilerParams(
            dimension_semantics=("parallel", "arbitrary"),
            vmem_limit_bytes=VMEM_LIMIT),
        name="inproj",
    )(xf, mod3, nw, w_main, w_hf, wga, aw, ab)


def _fast_masks():
    row = lax.broadcasted_iota(jnp.int32, (HALF, CHUNK), 0)
    col = lax.broadcasted_iota(jnp.int32, (HALF, CHUNK), 1)
    top = col <= row
    left = col < HALF
    bot = (col >= HALF) & (col <= row + HALF)
    return top, left, bot


def _block_cumsum8(g):
    g3 = g.reshape(NBLK, SUBLANES, g.shape[-1])
    sub = lax.broadcasted_iota(jnp.int32, g3.shape, 1)
    for sh in (1, 2, 4):
        g3 = g3 + jnp.where(sub >= sh, pltpu.roll(g3, sh, axis=1), 0.0)
    return g3


_NT = (((1,), (1,)), ((), ()))
_TN = (((0,), (0,)), ((), ()))


def _gla_chunk_fast(q, k, v, g, st, masks):
    top_mask, left_mask, bot_mask = masks
    c8a = _block_cumsum8(g)
    c8 = [c8a[i] for i in range(NBLK)]
    t8 = [ci[SUBLANES - 1:SUBLANES, :] for ci in c8]
    per = HALF // SUBLANES
    c32, t32 = [], []
    for half in range(2):
        run = None
        for i in range(half * per, (half + 1) * per):
            c32.append(c8[i] if run is None else c8[i] + run)
            run = t8[i] if run is None else run + t8[i]
        t32.append(run)
    c32 = jnp.concatenate(c32, axis=0)
    e0 = jnp.exp(t32[0])
    e1 = jnp.exp(t32[1])
    e01 = e0 * e1
    q32 = q * jnp.exp(c32)
    kd = k * jnp.exp(-c32)
    q32b = q32.astype(BF16)
    q64_hi = (q32[HALF:] * e0).astype(BF16)
    lhs = jnp.concatenate([q32b, q64_hi], axis=0)
    sc = lax.dot_general(lhs, kd.astype(BF16), _NT, preferred_element_type=F32)
    top = jnp.where(top_mask, sc[:HALF], 0.0)
    bot = jnp.where(left_mask, sc[CHUNK:], jnp.where(bot_mask, sc[HALF:CHUNK], 0.0))
    scores = jnp.concatenate([top, bot], axis=0).astype(BF16)

    vb = v.astype(BF16)
    q64 = jnp.concatenate([q32b[:HALF], q64_hi], axis=0)
    o = jnp.dot(scores, vb, preferred_element_type=F32)
    o = o + lax.dot_general(q64, st.astype(BF16), _NT, preferred_element_type=F32)
    k64 = jnp.concatenate([kd[:HALF] * e01, kd[HALF:] * e1], axis=0).astype(BF16)
    upd = lax.dot_general(vb, k64, _TN, preferred_element_type=F32)
    return o, st * e01 + upd


def _half_block_decay_ok(g):
    rows, cols = g.shape
    tot = jnp.sum(g.reshape(rows // HALF, HALF, cols), axis=1)
    return jnp.min(tot) >= -SAFE_DECAY


def _exact_masks():
    row = lax.broadcasted_iota(jnp.int32, (CHUNK, CHUNK), 0)
    col = lax.broadcasted_iota(jnp.int32, (CHUNK, CHUNK), 1)
    row_in = row & (SUBLANES - 1)
    row_base = row - row_in
    diag = [(col == row_base + s) & (row_in >= s) for s in range(SUBLANES)]
    levels = []
    for m in OFFDIAG_LEVELS:
        sh = m.bit_length() - 1
        tb = row >> sh
        sb = col >> sh
        levels.append(((tb & 1) == 1) & (sb == tb - 1))
    return diag, levels


def _gla_chunk_exact(q, k, v, g, st, masks):
    diag_masks, level_masks = masks
    kdim = q.shape[-1]
    c8a = _block_cumsum8(g)

    c = {8: [c8a[i] for i in range(NBLK)]}
    t = {8: [ci[SUBLANES - 1:SUBLANES, :] for ci in c[8]]}
    m = 8
    while m < CHUNK:
        per = m // SUBLANES
        nxt = []
        for i in range(NBLK):
            mb = i // per
            nxt.append(c[m][i] + t[m][mb - 1] if mb % 2 == 1 else c[m][i])
        c[2 * m] = nxt
        t[2 * m] = [t[m][2 * j] + t[m][2 * j + 1] for j in range(len(t[m]) // 2)]
        m *= 2

    def cat(blocks):
        return jnp.concatenate(blocks, axis=0)

    def q_scaled(m):
        return (q * jnp.exp(cat(c[m]))).astype(BF16)

    def k_scaled(m):
        per = m // SUBLANES
        rest = [t[m][i // per] - c[m][i] for i in range(NBLK)]
        return (k * jnp.exp(cat(rest))).astype(BF16)

    q3 = q.reshape(NBLK, SUBLANES, kdim)
    k3 = k.reshape(NBLK, SUBLANES, kdim)
    scores = jnp.zeros((CHUNK, CHUNK), F32)
    for s in range(SUBLANES):
        e = jnp.exp(jnp.minimum(c8a - c8a[:, s:s + 1, :], 0.0))
        p = q3 * e * k3[:, s:s + 1, :]
        r = jnp.sum(p, axis=-1, keepdims=True).reshape(CHUNK, 1)
        scores = scores + jnp.where(diag_masks[s], r, 0.0)
    for m, mask in zip(OFFDIAG_LEVELS, level_masks):
        sc = lax.dot_general(q_scaled(m), k_scaled(m), _NT, preferred_element_type=F32)
        scores = scores + jnp.where(mask, sc, 0.0)

    vb = v.astype(BF16)
    o = jnp.dot(scores.astype(BF16), vb, preferred_element_type=F32)
    o = o + lax.dot_general(q_scaled(CHUNK), st.astype(BF16), _NT, preferred_element_type=F32)
    upd = lax.dot_general(vb, k_scaled(CHUNK), _TN, preferred_element_type=F32)
    st_new = st * jnp.exp(t[CHUNK][0]) + upd
    return o, st_new


def _hgrn_scan_kernel(hq_ref, hf_ref, hi_ref, lb_ref, o_ref, st_ref, g_ref, k_ref, *, tb):
    @pl.when(pl.program_id(1) == 0)
    def _():
        st_ref[...] = jnp.zeros_like(st_ref)

    lb = jnp.clip(lb_ref[...], 0.0, 1.0)
    f = lb + (1.0 - lb) * jax.nn.sigmoid(hf_ref[...])
    g_ref[...] = jnp.log(jnp.maximum(f, F_FLOOR))
    k_ref[...] = 1.0 - f
    safe = _half_block_decay_ok(g_ref[...])

    def run(chunk_fn, masks, unroll=False):
        def body(ci, carry):
            r0 = pl.multiple_of(ci * CHUNK, CHUNK)
            rows = pl.ds(r0, CHUNK)
            for h in range(HGRN_HEADS):
                cols = slice(h * HEAD_K, (h + 1) * HEAD_K)
                o, st_new = chunk_fn(_silu(hq_ref[rows, cols].astype(F32)), k_ref[rows, cols],
                                     hi_ref[rows, cols].astype(F32), g_ref[rows, cols],
                                     st_ref[h], masks)
                st_ref[h] = st_new
                o_ref[rows, cols] = o.astype(o_ref.dtype)
            return carry

        lax.fori_loop(0, tb // CHUNK, body, 0, unroll=unroll)

    @pl.when(safe)
    def _():
        run(_gla_chunk_fast, _fast_masks(), unroll=True)

    @pl.when(jnp.logical_not(safe))
    def _():
        run(_gla_chunk_exact, _exact_masks())


def _gla_scan_kernel(qk_ref, gv_ref, loga_ref, o_ref, st_ref, *, tb):
    @pl.when(pl.program_id(1) == 0)
    def _():
        st_ref[...] = jnp.zeros_like(st_ref)

    scale = HEAD_K ** -0.5
    safe = _half_block_decay_ok(loga_ref[...])

    def run(chunk_fn, masks, unroll=False):
        def body(ci, carry):
            r0 = pl.multiple_of(ci * CHUNK, CHUNK)
            rows = pl.ds(r0, CHUNK)
            for h in range(GLA_HEADS):
                kc = slice(h * HEAD_K, (h + 1) * HEAD_K)
                kc2 = slice(GLA_KEY + h * HEAD_K, GLA_KEY + (h + 1) * HEAD_K)
                vc = slice(h * GLA_HEAD_V, (h + 1) * GLA_HEAD_V)
                o, st_new = chunk_fn(qk_ref[rows, kc].astype(F32) * scale,
                                     qk_ref[rows, kc2].astype(F32), gv_ref[rows, vc].astype(F32),
                                     loga_ref[rows, kc], st_ref[h], masks)
                st_ref[h] = st_new
                o_ref[rows, vc] = o.astype(o_ref.dtype)
            return carry

        lax.fori_loop(0, tb // CHUNK, body, 0, unroll=unroll)

    @pl.when(safe)
    def _():
        run(_gla_chunk_fast, _fast_masks(), unroll=True)

    @pl.when(jnp.logical_not(safe))
    def _():
        run(_gla_chunk_exact, _exact_masks())


def _scan_params():
    return pltpu.CompilerParams(dimension_semantics=("parallel", "arbitrary"),
                                vmem_limit_bytes=VMEM_LIMIT)


def _hgrn_scan(proj, hf, lb, *, batch, seq, tb=256):
    m = proj.shape[0]
    nt = seq // tb
    d = D_MODEL
    row = lambda b, t: b * nt + t
    return pl.pallas_call(
        functools.partial(_hgrn_scan_kernel, tb=tb),
        grid=(batch, nt),
        in_specs=[
            pl.BlockSpec((tb, PROJ_BLOCK), lambda b, t: (row(b, t), BLK_HQ)),
            pl.BlockSpec((tb, PROJ_BLOCK), lambda b, t: (row(b, t), 0)),
            pl.BlockSpec((tb, PROJ_BLOCK), lambda b, t: (row(b, t), BLK_HI)),
            pl.BlockSpec((1, d), lambda b, t: (0, 0)),
        ],
        out_specs=pl.BlockSpec((tb, d), lambda b, t: (row(b, t), 0)),
        out_shape=jax.ShapeDtypeStruct((m, d), BF16),
        scratch_shapes=[pltpu.VMEM((HGRN_HEADS, HGRN_HEAD_V, HEAD_K), F32),
                        pltpu.VMEM((tb, d), F32), pltpu.VMEM((tb, d), F32)],
        compiler_params=_scan_params(),
        name="hgrn_scan",
    )(proj, hf, proj, lb)


def _gla_scan(proj, loga, *, batch, seq, tb=256):
    m = proj.shape[0]
    nt = seq // tb
    d = D_MODEL
    row = lambda b, t: b * nt + t
    return pl.pallas_call(
        functools.partial(_gla_scan_kernel, tb=tb),
        grid=(batch, nt),
        in_specs=[
            pl.BlockSpec((tb, PROJ_BLOCK), lambda b, t: (row(b, t), BLK_GQK)),
            pl.BlockSpec((tb, PROJ_BLOCK), lambda b, t: (row(b, t), BLK_GV)),
            pl.BlockSpec((tb, GLA_KEY), lambda b, t: (row(b, t), 0)),
        ],
        out_specs=pl.BlockSpec((tb, d), lambda b, t: (row(b, t), 0)),
        out_shape=jax.ShapeDtypeStruct((m, d), BF16),
        scratch_shapes=[pltpu.VMEM((GLA_HEADS, GLA_HEAD_V, HEAD_K), F32)],
        compiler_params=_scan_params(),
        name="gla_scan",
    )(proj, proj, loga)


def _rms(x, w):
    return x * lax.rsqrt(jnp.mean(x * x, axis=-1, keepdims=True) + EPS) * w


def _out_kernel(oh_ref, og_ref, hz_ref, gz_ref, mg0_ref, mg1_ref, x_ref, mod_ref,
                hnw_ref, gnw_ref, wb0_ref, wb1_ref, wo_ref, fnw_ref, out_ref, *, final):
    f32 = lambda ref: ref[...].astype(F32)
    yh = _rms(f32(oh_ref), hnw_ref[...]) * _silu(f32(hz_ref))
    gnw = gnw_ref[...]
    og = jnp.concatenate(
        [_rms(og_ref[:, h * GLA_HEAD_V:(h + 1) * GLA_HEAD_V].astype(F32), gnw)
         for h in range(GLA_HEADS)], axis=-1)
    yg = og * _silu(f32(gz_ref))
    uh = jnp.dot(yh.astype(BF16), wb0_ref[...], preferred_element_type=F32)
    ug = jnp.dot(yg.astype(BF16), wb1_ref[...], preferred_element_type=F32)
    merged = jax.nn.sigmoid(f32(mg0_ref)) * uh + jax.nn.sigmoid(f32(mg1_ref)) * ug
    gate = mod_ref[0, 2:3, :]
    out = x_ref[...] + gate * jnp.dot(merged.astype(BF16), wo_ref[...],
                                      preferred_element_type=F32)
    if final:
        out = _rms(out, fnw_ref[...])
    out_ref[...] = out


def _outproj(oh, og, proj, xf, mod3, hnw, gnw, wb0, wb1, wo, fnw, *, seq, final, tm=256):
    m, d = xf.shape
    per_batch = seq // tm
    rows = lambda c: pl.BlockSpec((tm, PROJ_BLOCK), lambda i: (i, c))
    const = lambda shape: pl.BlockSpec(shape, lambda i: tuple(0 for _ in shape))
    return pl.pallas_call(
        functools.partial(_out_kernel, final=final),
        grid=(m // tm,),
        in_specs=[
            rows(0), rows(0), rows(BLK_HZ), rows(BLK_GZ), rows(BLK_MG0), rows(BLK_MG1), rows(0),
            pl.BlockSpec((1, 3, d), lambda i: (i // per_batch, 0, 0)),
            const((1, d)), const((1, GLA_HEAD_V)),
            const((d, d)), const((d, d)), const((d, d)), const((1, d)),
        ],
        out_specs=rows(0),
        out_shape=jax.ShapeDtypeStruct((m, d), F32),
        compiler_params=pltpu.CompilerParams(
            dimension_semantics=("parallel",), vmem_limit_bytes=VMEM_LIMIT),
        name="outproj",
    )(oh, og, proj, proj, proj, proj, xf, mod3, hnw, gnw, wb0, wb1, wo, fnw)


def kernel(x, c, ada_w, ada_b, norm_w, w_in, hgrn_lb_logits, hgrn_norm_w, gla_alpha_w,
           gla_alpha_b, gla_norm_w, w_branch, w_out, final_norm_w):
    batch, seq, d = x.shape
    depth = ada_w.shape[0]
    mod_all, lb_all = _prep(c, ada_w, ada_b, hgrn_lb_logits)
    xf = x.reshape(batch * seq, d)
    ga_hi = COL_GA + GLA_GATE_RANK
    for l in range(depth):
        w_main = jnp.concatenate([w_in[l][:, :COL_HF], w_in[l][:, COL_HI:COL_GA], w_in[l][:, ga_hi:]],
                                 axis=1).astype(BF16)
        w_hf = w_in[l][:, COL_HF:COL_HI].astype(BF16)
        wga = jnp.pad(w_in[l][:, COL_GA:ga_hi], ((0, 0), (0, LANES - GLA_GATE_RANK))).astype(BF16)
        aw = jnp.pad(gla_alpha_w[l], ((0, LANES - GLA_GATE_RANK), (0, 0))).astype(BF16)
        mod3 = mod_all[l].reshape(batch, 3, d)
        proj, hf, loga = _inproj(xf, mod3, norm_w[l][None], w_main, w_hf, wga, aw,
                             gla_alpha_b[l][None], seq=seq)
        oh = _hgrn_scan(proj, hf, lb_all[l][None], batch=batch, seq=seq)
        og = _gla_scan(proj, loga, batch=batch, seq=seq)
        xf = _outproj(oh, og, proj, xf, mod3, hgrn_norm_w[l][None], gla_norm_w[l][None],
                      w_branch[l, 0].astype(BF16), w_branch[l, 1].astype(BF16),
                      w_out[l].astype(BF16), final_norm_w[None],
                      seq=seq, final=(l == depth - 1))
    return xf.reshape(batch, seq, d)
```

```python
import functools

import jax
import jax.numpy as jnp
from jax import lax
from jax.experimental import pallas as pl
from jax.experimental.pallas import tpu as pltpu

F32 = jnp.float32
BF16 = jnp.bfloat16

D_MODEL = 1024
DEPTH = 2
HGRN_HEADS = 8
GLA_HEADS = 4
HEAD_K = 128
HGRN_HEAD_V = 128
GLA_HEAD_V = 256
GLA_KEY = GLA_HEADS * HEAD_K
GLA_GATE_RANK = 16
GLA_GATE_TAU = 16.0
EPS = 1e-6
F_FLOOR = 1e-20

LANES = 128
SUBLANES = 8
CHUNK = 64
NBLK = CHUNK // SUBLANES
HALF = CHUNK // 2
SAFE_DECAY = 60.0
OFFDIAG_LEVELS = (8, 16, 32)

COL_HF = 1024
COL_HI = 2048
COL_GA = 7168
PROJ_BLOCK = 1024
BLK_HQ, BLK_HI, BLK_HZ, BLK_GQK, BLK_GV, BLK_GZ, BLK_MG0, BLK_MG1 = range(8)
ACT_ID, ACT_SILU, ACT_SIGMOID = range(3)
BLOCK_ACT = (ACT_SILU, ACT_ID, ACT_SILU, ACT_ID, ACT_ID, ACT_SILU, ACT_SIGMOID, ACT_SIGMOID)
SCAN_VMEM_LIMIT = 48 * 1024 * 1024
INPROJ_VMEM_LIMIT = 56 * 1024 * 1024


def _silu(x):
    return x * jax.nn.sigmoid(x)


def _mod_kernel(c_ref, w_ref, b_ref, mod_ref):
    c = c_ref[...]
    c_act = _silu(c)
    mod_ref[0] = jnp.dot(c_act, w_ref[0], preferred_element_type=F32) + b_ref[0]


def _lb_kernel(logit_ref, lb_ref):
    x = logit_ref[...]
    m = jnp.max(x, axis=0, keepdims=True)
    e = jnp.exp(x - m)
    p = e / jnp.sum(e, axis=0, keepdims=True)
    run = jnp.zeros_like(p[0:1])
    for l in range(x.shape[0]):
        run = run + p[l:l + 1]
        lb_ref[l:l + 1, :] = run - p[0:1]


def _prep(c, ada_w, ada_b, lb_logits):
    depth, d, d3 = ada_w.shape
    b = c.shape[0]
    nblk = d3 // d
    mod = pl.pallas_call(
        _mod_kernel,
        grid=(depth, nblk),
        in_specs=[
            pl.BlockSpec((b, d), lambda l, n: (0, 0)),
            pl.BlockSpec((1, d, d), lambda l, n: (l, 0, n)),
            pl.BlockSpec((1, b, d), lambda l, n: (l, 0, n)),
        ],
        out_specs=pl.BlockSpec((1, b, d), lambda l, n: (l, 0, n)),
        out_shape=jax.ShapeDtypeStruct((depth, b, d3), F32),
        name="adaln_mod",
    )(c, ada_w, jnp.broadcast_to(ada_b[:, None, :], (depth, b, d3)))
    lb = pl.pallas_call(
        _lb_kernel,
        out_shape=jax.ShapeDtypeStruct(lb_logits.shape, F32),
        name="hgrn_lower_bounds",
    )(lb_logits)
    return mod, lb


def _half_block_sums(g):
    rows, cols = g.shape
    return jnp.sum(g.reshape(rows // HALF, HALF, cols), axis=1)


def _block_has_act(j, act):
    hits = [j == b for b, a in enumerate(BLOCK_ACT) if a == act]
    return functools.reduce(jnp.logical_or, hits)


def _inproj_kernel(x_ref, mod_ref, nw_ref, w_ref, whf_ref, lb_ref, wga_ref, aw_ref, ab_ref,
                   out_ref, g_ref, k_ref, gsum_ref, loga_ref, lsum_ref, h_ref):
    j = pl.program_id(1)

    @pl.when(j == 0)
    def _():
        x = x_ref[...]
        ms = jnp.mean(x * x, axis=-1, keepdims=True)
        y = x * lax.rsqrt(ms + EPS) * nw_ref[...]
        shift = mod_ref[0, 0:1, :]
        scale = mod_ref[0, 1:2, :]
        hb = (y * (1.0 + scale) + shift).astype(BF16)
        h_ref[...] = hb
        lb = jnp.clip(lb_ref[...], 0.0, 1.0)
        hf = jnp.dot(hb, whf_ref[...], preferred_element_type=F32)
        f = lb + (1.0 - lb) * jax.nn.sigmoid(hf)
        g = jnp.log(jnp.maximum(f, F_FLOOR))
        g_ref[...] = g
        k_ref[...] = (1.0 - f).astype(BF16)
        gsum_ref[...] = _half_block_sums(g)
        ga = jnp.dot(hb, wga_ref[...], preferred_element_type=F32)
        z = jnp.dot(ga.astype(BF16), aw_ref[...], preferred_element_type=F32) + ab_ref[...]
        log_sig = jnp.minimum(z, 0.0) - jnp.log1p(jnp.exp(-jnp.abs(z)))
        loga = log_sig * (1.0 / GLA_GATE_TAU)
        loga_ref[...] = loga
        lsum_ref[...] = _half_block_sums(loga)

    def project(act):
        acc = jnp.dot(h_ref[...], w_ref[...], preferred_element_type=F32)
        if act == ACT_SILU:
            acc = _silu(acc)
        elif act == ACT_SIGMOID:
            acc = jax.nn.sigmoid(acc)
        out_ref[...] = acc.astype(BF16)

    for act in (ACT_ID, ACT_SILU, ACT_SIGMOID):
        pl.when(_block_has_act(j, act))(functools.partial(project, act))


def _inproj(xf, mod3, nw, w_main, w_hf, lb, wga, aw, ab, *, seq, tm=1024):
    m, d = xf.shape
    n = w_main.shape[1]
    tn = PROJ_BLOCK
    per_batch = seq // tm
    const = lambda shape: pl.BlockSpec(shape, lambda i, j: (0, 0))
    rows = lambda r, c: pl.BlockSpec((r, c), lambda i, j: (i, 0))
    return pl.pallas_call(
        _inproj_kernel,
        grid=(m // tm, n // tn),
        in_specs=[
            rows(tm, d),
            pl.BlockSpec((1, 3, d), lambda i, j: (i // per_batch, 0, 0)),
            const((1, d)),
            pl.BlockSpec((d, tn), lambda i, j: (0, j)),
            const((d, PROJ_BLOCK)),
            const((1, d)),
            const((d, LANES)),
            const((LANES, GLA_KEY)),
            const((1, GLA_KEY)),
        ],
        out_specs=[
            pl.BlockSpec((tm, tn), lambda i, j: (i, j)),
            rows(tm, d), rows(tm, d), rows(tm // HALF, d),
            rows(tm, GLA_KEY), rows(tm // HALF, GLA_KEY),
        ],
        out_shape=[
            jax.ShapeDtypeStruct((m, n), BF16),
            jax.ShapeDtypeStruct((m, d), F32),
            jax.ShapeDtypeStruct((m, d), BF16),
            jax.ShapeDtypeStruct((m // HALF, d), F32),
            jax.ShapeDtypeStruct((m, GLA_KEY), F32),
            jax.ShapeDtypeStruct((m // HALF, GLA_KEY), F32),
        ],
        scratch_shapes=[pltpu.VMEM((tm, d), BF16)],
        compiler_params=pltpu.CompilerParams(
            dimension_semantics=("parallel", "arbitrary"),
            vmem_limit_bytes=INPROJ_VMEM_LIMIT),
        name="inproj",
    )(xf, mod3, nw, w_main, w_hf, lb, wga, aw, ab)


def _fast_masks():
    row = lax.broadcasted_iota(jnp.int32, (HALF, CHUNK), 0)
    col = lax.broadcasted_iota(jnp.int32, (HALF, CHUNK), 1)
    top = col <= row
    left = col < HALF
    bot = (col >= HALF) & (col <= row + HALF)
    return top, left, bot


def _block_cumsum8(g):
    g3 = g.reshape(NBLK, SUBLANES, g.shape[-1])
    sub = lax.broadcasted_iota(jnp.int32, g3.shape, 1)
    for sh in (1, 2, 4):
        g3 = g3 + jnp.where(sub >= sh, pltpu.roll(g3, sh, axis=1), 0.0)
    return g3


_NT = (((1,), (1,)), ((), ()))
_TN = (((0,), (0,)), ((), ()))


def _gla_chunk_fast(q, k, vb, g, st, masks):
    top_mask, left_mask, bot_mask = masks
    c8a = _block_cumsum8(g)
    c8 = [c8a[i] for i in range(NBLK)]
    t8 = [ci[SUBLANES - 1:SUBLANES, :] for ci in c8]
    per = HALF // SUBLANES
    c32, t32 = [], []
    for half in range(2):
        run = None
        for i in range(half * per, (half + 1) * per):
            c32.append(c8[i] if run is None else c8[i] + run)
            run = t8[i] if run is None else run + t8[i]
        t32.append(run)
    c32 = jnp.concatenate(c32, axis=0)
    e0 = jnp.exp(t32[0])
    e1 = jnp.exp(t32[1])
    e01 = e0 * e1
    q32 = q * jnp.exp(c32)
    kd = k * jnp.exp(-c32)
    q32b = q32.astype(BF16)
    q64_hi = (q32[HALF:] * e0).astype(BF16)
    lhs = jnp.concatenate([q32b, q64_hi], axis=0)
    sc = lax.dot_general(lhs, kd.astype(BF16), _NT, preferred_element_type=F32)
    top = jnp.where(top_mask, sc[:HALF], 0.0)
    bot = jnp.where(left_mask, sc[CHUNK:], jnp.where(bot_mask, sc[HALF:CHUNK], 0.0))
    scores = jnp.concatenate([top, bot], axis=0).astype(BF16)

    q64 = jnp.concatenate([q32b[:HALF], q64_hi], axis=0)
    o = jnp.dot(scores, vb, preferred_element_type=F32)
    o = o + lax.dot_general(q64, st.astype(BF16), _NT, preferred_element_type=F32)
    k64 = jnp.concatenate([kd[:HALF] * e01, kd[HALF:] * e1], axis=0).astype(BF16)
    upd = lax.dot_general(vb, k64, _TN, preferred_element_type=F32)
    return o, st * e01 + upd


def _exact_masks():
    row = lax.broadcasted_iota(jnp.int32, (CHUNK, CHUNK), 0)
    col = lax.broadcasted_iota(jnp.int32, (CHUNK, CHUNK), 1)
    row_in = row & (SUBLANES - 1)
    row_base = row - row_in
    diag = [(col == row_base + s) & (row_in >= s) for s in range(SUBLANES)]
    levels = []
    for m in OFFDIAG_LEVELS:
        sh = m.bit_length() - 1
        tb = row >> sh
        sb = col >> sh
        levels.append(((tb & 1) == 1) & (sb == tb - 1))
    return diag, levels


def _gla_chunk_exact(q, k, vb, g, st, masks):
    diag_masks, level_masks = masks
    kdim = q.shape[-1]
    c8a = _block_cumsum8(g)

    c = {8: [c8a[i] for i in range(NBLK)]}
    t = {8: [ci[SUBLANES - 1:SUBLANES, :] for ci in c[8]]}
    m = 8
    while m < CHUNK:
        per = m // SUBLANES
        nxt = []
        for i in range(NBLK):
            mb = i // per
            nxt.append(c[m][i] + t[m][mb - 1] if mb % 2 == 1 else c[m][i])
        c[2 * m] = nxt
        t[2 * m] = [t[m][2 * j] + t[m][2 * j + 1] for j in range(len(t[m]) // 2)]
        m *= 2

    def cat(blocks):
        return jnp.concatenate(blocks, axis=0)

    def q_scaled(m):
        return (q * jnp.exp(cat(c[m]))).astype(BF16)

    def k_scaled(m):
        per = m // SUBLANES
        rest = [t[m][i // per] - c[m][i] for i in range(NBLK)]
        return (k * jnp.exp(cat(rest))).astype(BF16)

    q3 = q.reshape(NBLK, SUBLANES, kdim)
    k3 = k.reshape(NBLK, SUBLANES, kdim)
    scores = jnp.zeros((CHUNK, CHUNK), F32)
    for s in range(SUBLANES):
        e = jnp.exp(jnp.minimum(c8a - c8a[:, s:s + 1, :], 0.0))
        p = q3 * e * k3[:, s:s + 1, :]
        r = jnp.sum(p, axis=-1, keepdims=True).reshape(CHUNK, 1)
        scores = scores + jnp.where(diag_masks[s], r, 0.0)
    for m, mask in zip(OFFDIAG_LEVELS, level_masks):
        sc = lax.dot_general(q_scaled(m), k_scaled(m), _NT, preferred_element_type=F32)
        scores = scores + jnp.where(mask, sc, 0.0)

    o = jnp.dot(scores.astype(BF16), vb, preferred_element_type=F32)
    o = o + lax.dot_general(q_scaled(CHUNK), st.astype(BF16), _NT, preferred_element_type=F32)
    upd = lax.dot_general(vb, k_scaled(CHUNK), _TN, preferred_element_type=F32)
    st_new = st * jnp.exp(t[CHUNK][0]) + upd
    return o, st_new


def _scan_step(load_qkg, v_ref, gsum_ref, o_ref, st_ref, *, tb, heads, head_v):
    @pl.when(pl.program_id(1) == 0)
    def _():
        st_ref[...] = jnp.zeros_like(st_ref)

    safe = jnp.min(gsum_ref[...]) >= -SAFE_DECAY

    def head_chunk(chunk_fn, rows, h, masks):
        kc = slice(h * HEAD_K, (h + 1) * HEAD_K)
        vc = slice(h * head_v, (h + 1) * head_v)
        q, k, g = load_qkg(rows, kc)
        o, st_new = chunk_fn(q, k, v_ref[rows, vc], g, st_ref[h], masks)
        st_ref[h] = st_new
        o_ref[rows, vc] = o.astype(o_ref.dtype)

    @pl.when(safe)
    def _():
        masks = _fast_masks()
        for c in range(tb // CHUNK):
            for h in range(heads):
                head_chunk(_gla_chunk_fast, pl.ds(c * CHUNK, CHUNK), h, masks)

    @pl.when(jnp.logical_not(safe))
    def _():
        masks = _exact_masks()

        def body(ci, carry):
            rows = pl.ds(pl.multiple_of(ci * CHUNK, CHUNK), CHUNK)
            for h in range(heads):
                head_chunk(_gla_chunk_exact, rows, h, masks)
            return carry

        lax.fori_loop(0, tb // CHUNK, body, 0)


def _hgrn_scan_kernel(q_ref, k_ref, v_ref, g_ref, gsum_ref, o_ref, st_ref, *, tb):
    def load_qkg(rows, kc):
        return q_ref[rows, kc].astype(F32), k_ref[rows, kc].astype(F32), g_ref[rows, kc]

    _scan_step(load_qkg, v_ref, gsum_ref, o_ref, st_ref, tb=tb, heads=HGRN_HEADS,
               head_v=HGRN_HEAD_V)


def _gla_scan_kernel(qk_ref, v_ref, g_ref, gsum_ref, o_ref, st_ref, *, tb):
    scale = HEAD_K ** -0.5

    def load_qkg(rows, kc):
        kc2 = slice(GLA_KEY + kc.start, GLA_KEY + kc.stop)
        return (qk_ref[rows, kc].astype(F32) * scale, qk_ref[rows, kc2].astype(F32),
                g_ref[rows, kc])

    _scan_step(load_qkg, v_ref, gsum_ref, o_ref, st_ref, tb=tb, heads=GLA_HEADS,
               head_v=GLA_HEAD_V)


def _scan_call(kernel_fn, operands, in_specs, *, name, m, tb, batch, nt, heads, head_v):
    d = D_MODEL
    return pl.pallas_call(
        functools.partial(kernel_fn, tb=tb),
        grid=(batch, nt),
        in_specs=in_specs,
        out_specs=pl.BlockSpec((tb, d), lambda b, t: (b * nt + t, 0)),
        out_shape=jax.ShapeDtypeStruct((m, d), BF16),
        scratch_shapes=[pltpu.VMEM((heads, head_v, HEAD_K), F32)],
        compiler_params=pltpu.CompilerParams(dimension_semantics=("parallel", "arbitrary"),
                                             vmem_limit_bytes=SCAN_VMEM_LIMIT),
        name=name,
    )(*operands)


def _hgrn_scan(proj, k, g, gsum, *, batch, seq, tb=512):
    nt = seq // tb
    blk = lambda r, c, col: pl.BlockSpec((r, c), lambda b, t: (b * nt + t, col))
    in_specs = [blk(tb, PROJ_BLOCK, BLK_HQ), blk(tb, PROJ_BLOCK, 0), blk(tb, PROJ_BLOCK, BLK_HI),
                blk(tb, PROJ_BLOCK, 0), blk(tb // HALF, PROJ_BLOCK, 0)]
    return _scan_call(_hgrn_scan_kernel, (proj, k, proj, g, gsum), in_specs, name="hgrn_scan",
                      m=proj.shape[0], tb=tb, batch=batch, nt=nt, heads=HGRN_HEADS,
                      head_v=HGRN_HEAD_V)


def _gla_scan(proj, loga, lsum, *, batch, seq, tb=512):
    nt = seq // tb
    blk = lambda r, c, col: pl.BlockSpec((r, c), lambda b, t: (b * nt + t, col))
    in_specs = [blk(tb, PROJ_BLOCK, BLK_GQK), blk(tb, PROJ_BLOCK, BLK_GV),
                blk(tb, GLA_KEY, 0), blk(tb // HALF, GLA_KEY, 0)]
    return _scan_call(_gla_scan_kernel, (proj, proj, loga, lsum), in_specs, name="gla_scan",
                      m=proj.shape[0], tb=tb, batch=batch, nt=nt, heads=GLA_HEADS,
                      head_v=GLA_HEAD_V)


def _rms(x, w):
    return x * lax.rsqrt(jnp.mean(x * x, axis=-1, keepdims=True) + EPS) * w


def _out_kernel(oh_ref, og_ref, hz_ref, gz_ref, mg0_ref, mg1_ref, x_ref, mod_ref,
                hnw_ref, gnw_ref, wb0_ref, wb1_ref, wo_ref, fnw_ref, out_ref, *, final):
    f32 = lambda ref: ref[...].astype(F32)
    yh = _rms(f32(oh_ref), hnw_ref[...]) * f32(hz_ref)
    gnw = gnw_ref[...]
    og = jnp.concatenate(
        [_rms(og_ref[:, h * GLA_HEAD_V:(h + 1) * GLA_HEAD_V].astype(F32), gnw)
         for h in range(GLA_HEADS)], axis=-1)
    yg = og * f32(gz_ref)
    uh = jnp.dot(yh.astype(BF16), wb0_ref[...], preferred_element_type=F32)
    ug = jnp.dot(yg.astype(BF16), wb1_ref[...], preferred_element_type=F32)
    merged = f32(mg0_ref) * uh + f32(mg1_ref) * ug
    gate = mod_ref[0, 2:3, :]
    out = x_ref[...] + gate * jnp.dot(merged.astype(BF16), wo_ref[...],
                                      preferred_element_type=F32)
    if final:
        out = _rms(out, fnw_ref[...])
    out_ref[...] = out


def _outproj(oh, og, proj, xf, mod3, hnw, gnw, wb0, wb1, wo, fnw, *, seq, final, tm=512):
    m, d = xf.shape
    per_batch = seq // tm
    rows = lambda c: pl.BlockSpec((tm, PROJ_BLOCK), lambda i: (i, c))
    const = lambda shape: pl.BlockSpec(shape, lambda i: tuple(0 for _ in shape))
    return pl.pallas_call(
        functools.partial(_out_kernel, final=final),
        grid=(m // tm,),
        in_specs=[
            rows(0), rows(0), rows(BLK_HZ), rows(BLK_GZ), rows(BLK_MG0), rows(BLK_MG1), rows(0),
            pl.BlockSpec((1, 3, d), lambda i: (i // per_batch, 0, 0)),
            const((1, d)), const((1, GLA_HEAD_V)),
            const((d, d)), const((d, d)), const((d, d)), const((1, d)),
        ],
        out_specs=rows(0),
        out_shape=jax.ShapeDtypeStruct((m, d), F32),
        compiler_params=pltpu.CompilerParams(
            dimension_semantics=("parallel",), vmem_limit_bytes=SCAN_VMEM_LIMIT),
        name="outproj",
    )(oh, og, proj, proj, proj, proj, xf, mod3, hnw, gnw, wb0, wb1, wo, fnw)


def kernel(x, c, ada_w, ada_b, norm_w, w_in, hgrn_lb_logits, hgrn_norm_w, gla_alpha_w,
           gla_alpha_b, gla_norm_w, w_branch, w_out, final_norm_w):
    batch, seq, d = x.shape
    depth = ada_w.shape[0]
    mod_all, lb_all = _prep(c, ada_w, ada_b, hgrn_lb_logits)
    xf = x.reshape(batch * seq, d)
    ga_hi = COL_GA + GLA_GATE_RANK
    for l in range(depth):
        w_main = jnp.concatenate([w_in[l][:, :COL_HF], w_in[l][:, COL_HI:COL_GA], w_in[l][:, ga_hi:]],
                                 axis=1).astype(BF16)
        w_hf = w_in[l][:, COL_HF:COL_HI].astype(BF16)
        wga = jnp.pad(w_in[l][:, COL_GA:ga_hi], ((0, 0), (0, LANES - GLA_GATE_RANK))).astype(BF16)
        aw = jnp.pad(gla_alpha_w[l], ((0, LANES - GLA_GATE_RANK), (0, 0))).astype(BF16)
        mod3 = mod_all[l].reshape(batch, 3, d)
        proj, g, k, gsum, loga, lsum = _inproj(xf, mod3, norm_w[l][None], w_main, w_hf,
                                               lb_all[l][None], wga, aw, gla_alpha_b[l][None],
                                               seq=seq)
        oh = _hgrn_scan(proj, k, g, gsum, batch=batch, seq=seq)
        og = _gla_scan(proj, loga, lsum, batch=batch, seq=seq)
        xf = _outproj(oh, og, proj, xf, mod3, hgrn_norm_w[l][None], gla_norm_w[l][None],
                      w_branch[l, 0].astype(BF16), w_branch[l, 1].astype(BF16),
                      w_out[l].astype(BF16), final_norm_w[None],
                      seq=seq, final=(l == depth - 1))
    return xf.reshape(batch, seq, d)
```

```python
import functools

import jax
import jax.numpy as jnp
from jax import lax
from jax.experimental import pallas as pl
from jax.experimental.pallas import tpu as pltpu

F32 = jnp.float32
BF16 = jnp.bfloat16

D_MODEL = 1024
DEPTH = 2
HGRN_HEADS = 8
GLA_HEADS = 4
HEAD_K = 128
HGRN_HEAD_V = 128
GLA_HEAD_V = 256
GLA_KEY = GLA_HEADS * HEAD_K
GLA_GATE_RANK = 16
GLA_GATE_TAU = 16.0
EPS = 1e-6
F_FLOOR = 1e-20

LANES = 128
SUBLANES = 8
CHUNK = 64
NBLK = CHUNK // SUBLANES
HALF = CHUNK // 2
SAFE_DECAY = 60.0
OFFDIAG_LEVELS = (8, 16, 32)

COL_HF = 1024
COL_HI = 2048
COL_GA = 7168
PROJ_BLOCK = 1024
BLK_HQ, BLK_HI, BLK_HZ, BLK_GQK, BLK_GV, BLK_GZ, BLK_MG0, BLK_MG1 = range(8)
ACT_ID, ACT_SILU, ACT_SIGMOID = range(3)
BLOCK_ACT = (ACT_SILU, ACT_ID, ACT_SILU, ACT_ID, ACT_ID, ACT_SILU, ACT_SIGMOID, ACT_SIGMOID)
SCAN_VMEM_LIMIT = 48 * 1024 * 1024
INPROJ_VMEM_LIMIT = 56 * 1024 * 1024


def _silu(x):
    return x * jax.nn.sigmoid(x)


def _mod_kernel(c_ref, w_ref, b_ref, mod_ref):
    c = c_ref[...]
    c_act = _silu(c)
    mod_ref[0] = jnp.dot(c_act, w_ref[0], preferred_element_type=F32) + b_ref[0]


def _lb_kernel(logit_ref, lb_ref):
    x = logit_ref[...]
    m = jnp.max(x, axis=0, keepdims=True)
    e = jnp.exp(x - m)
    p = e / jnp.sum(e, axis=0, keepdims=True)
    run = jnp.zeros_like(p[0:1])
    for l in range(x.shape[0]):
        run = run + p[l:l + 1]
        lb_ref[l:l + 1, :] = run - p[0:1]


def _prep(c, ada_w, ada_b, lb_logits):
    depth, d, d3 = ada_w.shape
    b = c.shape[0]
    nblk = d3 // d
    mod = pl.pallas_call(
        _mod_kernel,
        grid=(depth, nblk),
        in_specs=[
            pl.BlockSpec((b, d), lambda l, n: (0, 0)),
            pl.BlockSpec((1, d, d), lambda l, n: (l, 0, n)),
            pl.BlockSpec((1, b, d), lambda l, n: (l, 0, n)),
        ],
        out_specs=pl.BlockSpec((1, b, d), lambda l, n: (l, 0, n)),
        out_shape=jax.ShapeDtypeStruct((depth, b, d3), F32),
        name="adaln_mod",
    )(c, ada_w, jnp.broadcast_to(ada_b[:, None, :], (depth, b, d3)))
    lb = pl.pallas_call(
        _lb_kernel,
        out_shape=jax.ShapeDtypeStruct(lb_logits.shape, F32),
        name="hgrn_lower_bounds",
    )(lb_logits)
    return mod, lb


def _half_block_sums(g):
    rows, cols = g.shape
    return jnp.sum(g.reshape(rows // HALF, HALF, cols), axis=1)


def _step_acts(per_step):
    return [BLOCK_ACT[s:s + per_step] for s in range(0, len(BLOCK_ACT), per_step)]


def _step_has_acts(j, acts, per_step):
    hits = [j == s for s, a in enumerate(_step_acts(per_step)) if a == acts]
    return functools.reduce(jnp.logical_or, hits)


def _inproj_kernel(x_ref, mod_ref, nw_ref, w_ref, whf_ref, lb_ref, wga_ref, aw_ref, ab_ref,
                   out_ref, g_ref, k_ref, gsum_ref, loga_ref, lsum_ref, h_ref):
    j = pl.program_id(1)

    @pl.when(j == 0)
    def _():
        x = x_ref[...]
        ms = jnp.mean(x * x, axis=-1, keepdims=True)
        y = x * lax.rsqrt(ms + EPS) * nw_ref[...]
        shift = mod_ref[0, 0:1, :]
        scale = mod_ref[0, 1:2, :]
        hb = (y * (1.0 + scale) + shift).astype(BF16)
        h_ref[...] = hb
        lb = jnp.clip(lb_ref[...], 0.0, 1.0)
        hf = jnp.dot(hb, whf_ref[...], preferred_element_type=F32)
        f = lb + (1.0 - lb) * jax.nn.sigmoid(hf)
        g = jnp.log(jnp.maximum(f, F_FLOOR))
        g_ref[...] = g
        k_ref[...] = (1.0 - f).astype(BF16)
        gsum_ref[...] = _half_block_sums(g)
        ga = jnp.dot(hb, wga_ref[...], preferred_element_type=F32)
        z = jnp.dot(ga.astype(BF16), aw_ref[...], preferred_element_type=F32) + ab_ref[...]
        log_sig = jnp.minimum(z, 0.0) - jnp.log1p(jnp.exp(-jnp.abs(z)))
        loga = log_sig * (1.0 / GLA_GATE_TAU)
        loga_ref[...] = loga
        lsum_ref[...] = _half_block_sums(loga)

    per_step = out_ref.shape[1] // PROJ_BLOCK

    def project(acts):
        for s, act in enumerate(acts):
            cols = slice(s * PROJ_BLOCK, (s + 1) * PROJ_BLOCK)
            acc = jnp.dot(h_ref[...], w_ref[:, cols], preferred_element_type=F32)
            if act != ACT_ID:
                sig = 0.5 * jnp.tanh(0.5 * acc) + 0.5
                acc = acc * sig if act == ACT_SILU else sig
            out_ref[:, cols] = acc.astype(BF16)

    for acts in sorted(set(_step_acts(per_step))):
        pl.when(_step_has_acts(j, acts, per_step))(functools.partial(project, acts))


def _inproj(xf, mod3, nw, w_main, w_hf, lb, wga, aw, ab, *, seq, tm=1024, tn=2 * PROJ_BLOCK):
    m, d = xf.shape
    n = w_main.shape[1]
    per_batch = seq // tm
    const = lambda shape: pl.BlockSpec(shape, lambda i, j: (0, 0))
    rows = lambda r, c: pl.BlockSpec((r, c), lambda i, j: (i, 0))
    return pl.pallas_call(
        _inproj_kernel,
        grid=(m // tm, n // tn),
        in_specs=[
            rows(tm, d),
            pl.BlockSpec((1, 3, d), lambda i, j: (i // per_batch, 0, 0)),
            const((1, d)),
            pl.BlockSpec((d, tn), lambda i, j: (0, j)),
            const((d, PROJ_BLOCK)),
            const((1, d)),
            const((d, LANES)),
            const((LANES, GLA_KEY)),
            const((1, GLA_KEY)),
        ],
        out_specs=[
            pl.BlockSpec((tm, tn), lambda i, j: (i, j)),
            rows(tm, d), rows(tm, d), rows(tm // HALF, d),
            rows(tm, GLA_KEY), rows(tm // HALF, GLA_KEY),
        ],
        out_shape=[
            jax.ShapeDtypeStruct((m, n), BF16),
            jax.ShapeDtypeStruct((m, d), F32),
            jax.ShapeDtypeStruct((m, d), BF16),
            jax.ShapeDtypeStruct((m // HALF, d), F32),
            jax.ShapeDtypeStruct((m, GLA_KEY), F32),
            jax.ShapeDtypeStruct((m // HALF, GLA_KEY), F32),
        ],
        scratch_shapes=[pltpu.VMEM((tm, d), BF16)],
        compiler_params=pltpu.CompilerParams(
            dimension_semantics=("parallel", "arbitrary"),
            vmem_limit_bytes=INPROJ_VMEM_LIMIT),
        name="inproj",
    )(xf, mod3, nw, w_main, w_hf, lb, wga, aw, ab)


def _fast_masks():
    row = lax.broadcasted_iota(jnp.int32, (HALF, CHUNK), 0)
    col = lax.broadcasted_iota(jnp.int32, (HALF, CHUNK), 1)
    top = col <= row
    left = col < HALF
    bot = (col >= HALF) & (col <= row + HALF)
    return top, left, bot


def _block_cumsum8(g):
    g3 = g.reshape(NBLK, SUBLANES, g.shape[-1])
    sub = lax.broadcasted_iota(jnp.int32, g3.shape, 1)
    for sh in (1, 2, 4):
        g3 = g3 + jnp.where(sub >= sh, pltpu.roll(g3, sh, axis=1), 0.0)
    return g3


_NT = (((1,), (1,)), ((), ()))
_TN = (((0,), (0,)), ((), ()))


def _gla_chunk_fast(q, k, vb, g, st, masks):
    top_mask, left_mask, bot_mask = masks
    c8a = _block_cumsum8(g)
    c8 = [c8a[i] for i in range(NBLK)]
    t8 = [ci[SUBLANES - 1:SUBLANES, :] for ci in c8]
    per = HALF // SUBLANES
    c32, t32 = [], []
    for half in range(2):
        run = None
        for i in range(half * per, (half + 1) * per):
            c32.append(c8[i] if run is None else c8[i] + run)
            run = t8[i] if run is None else run + t8[i]
        t32.append(run)
    c32 = jnp.concatenate(c32, axis=0)
    e0 = jnp.exp(t32[0])
    e1 = jnp.exp(t32[1])
    e01 = e0 * e1
    q32 = q * jnp.exp(c32)
    kd = k * jnp.exp(-c32)
    q32b = q32.astype(BF16)
    q64_hi = (q32[HALF:] * e0).astype(BF16)
    lhs = jnp.concatenate([q32b, q64_hi], axis=0)
    sc = lax.dot_general(lhs, kd.astype(BF16), _NT, preferred_element_type=F32)
    top = jnp.where(top_mask, sc[:HALF], 0.0)
    bot = jnp.where(left_mask, sc[CHUNK:], jnp.where(bot_mask, sc[HALF:CHUNK], 0.0))
    scores = jnp.concatenate([top, bot], axis=0).astype(BF16)

    q64 = jnp.concatenate([q32b[:HALF], q64_hi], axis=0)
    o = jnp.dot(scores, vb, preferred_element_type=F32)
    o = o + lax.dot_general(q64, st.astype(BF16), _NT, preferred_element_type=F32)
    k64 = jnp.concatenate([kd[:HALF] * e01, kd[HALF:] * e1], axis=0).astype(BF16)
    upd = lax.dot_general(vb, k64, _TN, preferred_element_type=F32)
    return o, st * e01 + upd


def _exact_masks():
    row = lax.broadcasted_iota(jnp.int32, (CHUNK, CHUNK), 0)
    col = lax.broadcasted_iota(jnp.int32, (CHUNK, CHUNK), 1)
    row_in = row & (SUBLANES - 1)
    row_base = row - row_in
    diag = [(col == row_base + s) & (row_in >= s) for s in range(SUBLANES)]
    levels = []
    for m in OFFDIAG_LEVELS:
        sh = m.bit_length() - 1
        tb = row >> sh
        sb = col >> sh
        levels.append(((tb & 1) == 1) & (sb == tb - 1))
    return diag, levels


def _gla_chunk_exact(q, k, vb, g, st, masks):
    diag_masks, level_masks = masks
    kdim = q.shape[-1]
    c8a = _block_cumsum8(g)

    c = {8: [c8a[i] for i in range(NBLK)]}
    t = {8: [ci[SUBLANES - 1:SUBLANES, :] for ci in c[8]]}
    m = 8
    while m < CHUNK:
        per = m // SUBLANES
        nxt = []
        for i in range(NBLK):
            mb = i // per
            nxt.append(c[m][i] + t[m][mb - 1] if mb % 2 == 1 else c[m][i])
        c[2 * m] = nxt
        t[2 * m] = [t[m][2 * j] + t[m][2 * j + 1] for j in range(len(t[m]) // 2)]
        m *= 2

    def cat(blocks):
        return jnp.concatenate(blocks, axis=0)

    def q_scaled(m):
        return (q * jnp.exp(cat(c[m]))).astype(BF16)

    def k_scaled(m):
        per = m // SUBLANES
        rest = [t[m][i // per] - c[m][i] for i in range(NBLK)]
        return (k * jnp.exp(cat(rest))).astype(BF16)

    q3 = q.reshape(NBLK, SUBLANES, kdim)
    k3 = k.reshape(NBLK, SUBLANES, kdim)
    scores = jnp.zeros((CHUNK, CHUNK), F32)
    for s in range(SUBLANES):
        e = jnp.exp(jnp.minimum(c8a - c8a[:, s:s + 1, :], 0.0))
        p = q3 * e * k3[:, s:s + 1, :]
        r = jnp.sum(p, axis=-1, keepdims=True).reshape(CHUNK, 1)
        scores = scores + jnp.where(diag_masks[s], r, 0.0)
    for m, mask in zip(OFFDIAG_LEVELS, level_masks):
        sc = lax.dot_general(q_scaled(m), k_scaled(m), _NT, preferred_element_type=F32)
        scores = scores + jnp.where(mask, sc, 0.0)

    o = jnp.dot(scores.astype(BF16), vb, preferred_element_type=F32)
    o = o + lax.dot_general(q_scaled(CHUNK), st.astype(BF16), _NT, preferred_element_type=F32)
    upd = lax.dot_general(vb, k_scaled(CHUNK), _TN, preferred_element_type=F32)
    st_new = st * jnp.exp(t[CHUNK][0]) + upd
    return o, st_new


def _scan_step(load_qkg, v_ref, gsum_ref, o_ref, st_ref, *, tb, heads, head_v):
    @pl.when(pl.program_id(1) == 0)
    def _():
        st_ref[...] = jnp.zeros_like(st_ref)

    safe = jnp.min(gsum_ref[...]) >= -SAFE_DECAY

    def head_chunk(chunk_fn, rows, h, masks):
        kc = slice(h * HEAD_K, (h + 1) * HEAD_K)
        vc = slice(h * head_v, (h + 1) * head_v)
        q, k, g = load_qkg(rows, kc)
        o, st_new = chunk_fn(q, k, v_ref[rows, vc], g, st_ref[h], masks)
        st_ref[h] = st_new
        o_ref[rows, vc] = o.astype(o_ref.dtype)

    @pl.when(safe)
    def _():
        masks = _fast_masks()
        for c in range(tb // CHUNK):
            for h in range(heads):
                head_chunk(_gla_chunk_fast, pl.ds(c * CHUNK, CHUNK), h, masks)

    @pl.when(jnp.logical_not(safe))
    def _():
        masks = _exact_masks()

        def body(ci, carry):
            rows = pl.ds(pl.multiple_of(ci * CHUNK, CHUNK), CHUNK)
            for h in range(heads):
                head_chunk(_gla_chunk_exact, rows, h, masks)
            return carry

        lax.fori_loop(0, tb // CHUNK, body, 0)


def _hgrn_scan_kernel(q_ref, k_ref, v_ref, g_ref, gsum_ref, o_ref, st_ref, *, tb):
    def load_qkg(rows, kc):
        return q_ref[rows, kc].astype(F32), k_ref[rows, kc].astype(F32), g_ref[rows, kc]

    _scan_step(load_qkg, v_ref, gsum_ref, o_ref, st_ref, tb=tb, heads=HGRN_HEADS,
               head_v=HGRN_HEAD_V)


def _gla_scan_kernel(qk_ref, v_ref, g_ref, gsum_ref, o_ref, st_ref, *, tb):
    scale = HEAD_K ** -0.5

    def load_qkg(rows, kc):
        kc2 = slice(GLA_KEY + kc.start, GLA_KEY + kc.stop)
        return (qk_ref[rows, kc].astype(F32) * scale, qk_ref[rows, kc2].astype(F32),
                g_ref[rows, kc])

    _scan_step(load_qkg, v_ref, gsum_ref, o_ref, st_ref, tb=tb, heads=GLA_HEADS,
               head_v=GLA_HEAD_V)


def _scan_call(kernel_fn, operands, in_specs, *, name, m, tb, batch, nt, heads, head_v):
    d = D_MODEL
    return pl.pallas_call(
        functools.partial(kernel_fn, tb=tb),
        grid=(batch, nt),
        in_specs=in_specs,
        out_specs=pl.BlockSpec((tb, d), lambda b, t: (b * nt + t, 0)),
        out_shape=jax.ShapeDtypeStruct((m, d), BF16),
        scratch_shapes=[pltpu.VMEM((heads, head_v, HEAD_K), F32)],
        compiler_params=pltpu.CompilerParams(dimension_semantics=("parallel", "arbitrary"),
                                             vmem_limit_bytes=SCAN_VMEM_LIMIT),
        name=name,
    )(*operands)


def _hgrn_scan(proj, k, g, gsum, *, batch, seq, tb=1024):
    nt = seq // tb
    blk = lambda r, c, col: pl.BlockSpec((r, c), lambda b, t: (b * nt + t, col))
    in_specs = [blk(tb, PROJ_BLOCK, BLK_HQ), blk(tb, PROJ_BLOCK, 0), blk(tb, PROJ_BLOCK, BLK_HI),
                blk(tb, PROJ_BLOCK, 0), blk(tb // HALF, PROJ_BLOCK, 0)]
    return _scan_call(_hgrn_scan_kernel, (proj, k, proj, g, gsum), in_specs, name="hgrn_scan",
                      m=proj.shape[0], tb=tb, batch=batch, nt=nt, heads=HGRN_HEADS,
                      head_v=HGRN_HEAD_V)


def _gla_scan(proj, loga, lsum, *, batch, seq, tb=1024):
    nt = seq // tb
    blk = lambda r, c, col: pl.BlockSpec((r, c), lambda b, t: (b * nt + t, col))
    in_specs = [blk(tb, PROJ_BLOCK, BLK_GQK), blk(tb, PROJ_BLOCK, BLK_GV),
                blk(tb, GLA_KEY, 0), blk(tb // HALF, GLA_KEY, 0)]
    return _scan_call(_gla_scan_kernel, (proj, proj, loga, lsum), in_specs, name="gla_scan",
                      m=proj.shape[0], tb=tb, batch=batch, nt=nt, heads=GLA_HEADS,
                      head_v=GLA_HEAD_V)


def _rms(x, w):
    return x * lax.rsqrt(jnp.mean(x * x, axis=-1, keepdims=True) + EPS) * w


def _out_kernel(oh_ref, og_ref, hz_ref, gz_ref, mg0_ref, mg1_ref, x_ref, mod_ref,
                hnw_ref, gnw_ref, wb0_ref, wb1_ref, wo_ref, fnw_ref, out_ref, *, final):
    f32 = lambda ref: ref[...].astype(F32)
    yh = _rms(f32(oh_ref), hnw_ref[...]) * f32(hz_ref)
    gnw = gnw_ref[...]
    og = jnp.concatenate(
        [_rms(og_ref[:, h * GLA_HEAD_V:(h + 1) * GLA_HEAD_V].astype(F32), gnw)
         for h in range(GLA_HEADS)], axis=-1)
    yg = og * f32(gz_ref)
    uh = jnp.dot(yh.astype(BF16), wb0_ref[...], preferred_element_type=F32)
    ug = jnp.dot(yg.astype(BF16), wb1_ref[...], preferred_element_type=F32)
    merged = f32(mg0_ref) * uh + f32(mg1_ref) * ug
    gate = mod_ref[0, 2:3, :]
    out = x_ref[...] + gate * jnp.dot(merged.astype(BF16), wo_ref[...],
                                      preferred_element_type=F32)
    if final:
        out = _rms(out, fnw_ref[...])
    out_ref[...] = out


def _outproj(oh, og, proj, xf, mod3, hnw, gnw, wb0, wb1, wo, fnw, *, seq, final, tm=512):
    m, d = xf.shape
    per_batch = seq // tm
    rows = lambda c: pl.BlockSpec((tm, PROJ_BLOCK), lambda i: (i, c))
    const = lambda shape: pl.BlockSpec(shape, lambda i: tuple(0 for _ in shape))
    return pl.pallas_call(
        functools.partial(_out_kernel, final=final),
        grid=(m // tm,),
        in_specs=[
            rows(0), rows(0), rows(BLK_HZ), rows(BLK_GZ), rows(BLK_MG0), rows(BLK_MG1), rows(0),
            pl.BlockSpec((1, 3, d), lambda i: (i // per_batch, 0, 0)),
            const((1, d)), const((1, GLA_HEAD_V)),
            const((d, d)), const((d, d)), const((d, d)), const((1, d)),
        ],
        out_specs=rows(0),
        out_shape=jax.ShapeDtypeStruct((m, d), F32),
        compiler_params=pltpu.CompilerParams(
            dimension_semantics=("parallel",), vmem_limit_bytes=SCAN_VMEM_LIMIT),
        name="outproj",
    )(oh, og, proj, proj, proj, proj, xf, mod3, hnw, gnw, wb0, wb1, wo, fnw)


def kernel(x, c, ada_w, ada_b, norm_w, w_in, hgrn_lb_logits, hgrn_norm_w, gla_alpha_w,
           gla_alpha_b, gla_norm_w, w_branch, w_out, final_norm_w):
    batch, seq, d = x.shape
    depth = ada_w.shape[0]
    mod_all, lb_all = _prep(c, ada_w, ada_b, hgrn_lb_logits)
    xf = x.reshape(batch * seq, d)
    ga_hi = COL_GA + GLA_GATE_RANK
    for l in range(depth):
        w_main = jnp.concatenate([w_in[l][:, :COL_HF], w_in[l][:, COL_HI:COL_GA], w_in[l][:, ga_hi:]],
                                 axis=1).astype(BF16)
        w_hf = w_in[l][:, COL_HF:COL_HI].astype(BF16)
        wga = jnp.pad(w_in[l][:, COL_GA:ga_hi], ((0, 0), (0, LANES - GLA_GATE_RANK))).astype(BF16)
        aw = jnp.pad(gla_alpha_w[l], ((0, LANES - GLA_GATE_RANK), (0, 0))).astype(BF16)
        mod3 = mod_all[l].reshape(batch, 3, d)
        proj, g, k, gsum, loga, lsum = _inproj(xf, mod3, norm_w[l][None], w_main, w_hf,
                                               lb_all[l][None], wga, aw, gla_alpha_b[l][None],
                                               seq=seq)
        oh = _hgrn_scan(proj, k, g, gsum, batch=batch, seq=seq)
        og = _gla_scan(proj, loga, lsum, batch=batch, seq=seq)
        xf = _outproj(oh, og, proj, xf, mod3, hgrn_norm_w[l][None], gla_norm_w[l][None],
                      w_branch[l, 0].astype(BF16), w_branch[l, 1].astype(BF16),
                      w_out[l].astype(BF16), final_norm_w[None],
                      seq=seq, final=(l == depth - 1))
    return xf.reshape(batch, seq, d)
```

```python
import functools

import jax
import jax.numpy as jnp
from jax import lax
from jax.experimental import pallas as pl
from jax.experimental.pallas import tpu as pltpu

F32 = jnp.float32
BF16 = jnp.bfloat16

D_MODEL = 1024
DEPTH = 2
HGRN_HEADS = 8
GLA_HEADS = 4
HEAD_K = 128
HGRN_HEAD_V = 128
GLA_HEAD_V = 256
GLA_KEY = GLA_HEADS * HEAD_K
GLA_GATE_RANK = 16
GLA_GATE_TAU = 16.0
EPS = 1e-6
F_FLOOR = 1e-20

LANES = 128
SUBLANES = 8
CHUNK = 64
NBLK = CHUNK // SUBLANES
HALF = CHUNK // 2
HGRN_HEAD_GROUP = 2
GLA_HEAD_GROUP = 1
SAFE_DECAY = 60.0
OFFDIAG_LEVELS = (8, 16, 32)

COL_HF = 1024
COL_HI = 2048
COL_GA = 7168
PROJ_BLOCK = 1024
BLK_HQ, BLK_HI, BLK_HZ, BLK_GQK, BLK_GV, BLK_GZ, BLK_MG0, BLK_MG1 = range(8)
ACT_ID, ACT_SILU, ACT_SIGMOID = range(3)
BLOCK_ACT = (ACT_SILU, ACT_ID, ACT_SILU, ACT_ID, ACT_ID, ACT_SILU, ACT_SIGMOID, ACT_SIGMOID)
SCAN_VMEM_LIMIT = 48 * 1024 * 1024
INPROJ_VMEM_LIMIT = 56 * 1024 * 1024


def _silu(x):
    return x * jax.nn.sigmoid(x)


def _mod_kernel(c_ref, w_ref, b_ref, mod_ref):
    c = c_ref[...]
    c_act = _silu(c)
    mod_ref[0] = jnp.dot(c_act, w_ref[0], preferred_element_type=F32) + b_ref[0]


def _lb_kernel(logit_ref, lb_ref):
    x = logit_ref[...]
    m = jnp.max(x, axis=0, keepdims=True)
    e = jnp.exp(x - m)
    p = e / jnp.sum(e, axis=0, keepdims=True)
    run = jnp.zeros_like(p[0:1])
    for l in range(x.shape[0]):
        run = run + p[l:l + 1]
        lb_ref[l:l + 1, :] = run - p[0:1]


def _prep(c, ada_w, ada_b, lb_logits):
    depth, d, d3 = ada_w.shape
    b = c.shape[0]
    nblk = d3 // d
    mod = pl.pallas_call(
        _mod_kernel,
        grid=(depth, nblk),
        in_specs=[
            pl.BlockSpec((b, d), lambda l, n: (0, 0)),
            pl.BlockSpec((1, d, d), lambda l, n: (l, 0, n)),
            pl.BlockSpec((1, b, d), lambda l, n: (l, 0, n)),
        ],
        out_specs=pl.BlockSpec((1, b, d), lambda l, n: (l, 0, n)),
        out_shape=jax.ShapeDtypeStruct((depth, b, d3), F32),
        name="adaln_mod",
    )(c, ada_w, jnp.broadcast_to(ada_b[:, None, :], (depth, b, d3)))
    lb = pl.pallas_call(
        _lb_kernel,
        out_shape=jax.ShapeDtypeStruct(lb_logits.shape, F32),
        name="hgrn_lower_bounds",
    )(lb_logits)
    return mod, lb


def _half_block_sums(g):
    rows, cols = g.shape
    return jnp.sum(g.reshape(rows // HALF, HALF, cols), axis=1)


def _step_acts(per_step):
    return [BLOCK_ACT[s:s + per_step] for s in range(0, len(BLOCK_ACT), per_step)]


def _step_has_acts(j, acts, per_step):
    hits = [j == s for s, a in enumerate(_step_acts(per_step)) if a == acts]
    return functools.reduce(jnp.logical_or, hits)


def _inproj_kernel(x_ref, mod_ref, nw_ref, w_ref, whf_ref, lb_ref, wga_ref, aw_ref, ab_ref,
                   out_ref, g_ref, k_ref, gsum_ref, loga_ref, lsum_ref, h_ref):
    j = pl.program_id(1)

    @pl.when(j == 0)
    def _():
        x = x_ref[...]
        ms = jnp.mean(x * x, axis=-1, keepdims=True)
        y = x * lax.rsqrt(ms + EPS) * nw_ref[...]
        shift = mod_ref[0, 0:1, :]
        scale = mod_ref[0, 1:2, :]
        hb = (y * (1.0 + scale) + shift).astype(BF16)
        h_ref[...] = hb
        lb = jnp.clip(lb_ref[...], 0.0, 1.0)
        hf = jnp.dot(hb, whf_ref[...], preferred_element_type=F32)
        f = lb + (1.0 - lb) * jax.nn.sigmoid(hf)
        g = jnp.log(jnp.maximum(f, F_FLOOR))
        g_ref[...] = g
        k_ref[...] = (1.0 - f).astype(BF16)
        gsum_ref[...] = _half_block_sums(g)
        ga = jnp.dot(hb, wga_ref[...], preferred_element_type=F32)
        z = jnp.dot(ga.astype(BF16), aw_ref[...], preferred_element_type=F32) + ab_ref[...]
        log_sig = jnp.minimum(z, 0.0) - jnp.log1p(jnp.exp(-jnp.abs(z)))
        loga = log_sig * (1.0 / GLA_GATE_TAU)
        loga_ref[...] = loga
        lsum_ref[...] = _half_block_sums(loga)

    per_step = out_ref.shape[1] // PROJ_BLOCK

    def project(acts):
        for s, act in enumerate(acts):
            cols = slice(s * PROJ_BLOCK, (s + 1) * PROJ_BLOCK)
            acc = jnp.dot(h_ref[...], w_ref[:, cols], preferred_element_type=F32)
            if act != ACT_ID:
                sig = 0.5 * jnp.tanh(0.5 * acc) + 0.5
                acc = acc * sig if act == ACT_SILU else sig
            out_ref[:, cols] = acc.astype(BF16)

    for acts in sorted(set(_step_acts(per_step))):
        pl.when(_step_has_acts(j, acts, per_step))(functools.partial(project, acts))


def _inproj(xf, mod3, nw, w_main, w_hf, lb, wga, aw, ab, *, seq, tm=1024, tn=2 * PROJ_BLOCK):
    m, d = xf.shape
    n = w_main.shape[1]
    per_batch = seq // tm
    const = lambda shape: pl.BlockSpec(shape, lambda i, j: (0, 0))
    rows = lambda r, c: pl.BlockSpec((r, c), lambda i, j: (i, 0))
    return pl.pallas_call(
        _inproj_kernel,
        grid=(m // tm, n // tn),
        in_specs=[
            rows(tm, d),
            pl.BlockSpec((1, 3, d), lambda i, j: (i // per_batch, 0, 0)),
            const((1, d)),
            pl.BlockSpec((d, tn), lambda i, j: (0, j)),
            const((d, PROJ_BLOCK)),
            const((1, d)),
            const((d, LANES)),
            const((LANES, GLA_KEY)),
            const((1, GLA_KEY)),
        ],
        out_specs=[
            pl.BlockSpec((tm, tn), lambda i, j: (i, j)),
            rows(tm, d), rows(tm, d), rows(tm // HALF, d),
            rows(tm, GLA_KEY), rows(tm // HALF, GLA_KEY),
        ],
        out_shape=[
            jax.ShapeDtypeStruct((m, n), BF16),
            jax.ShapeDtypeStruct((m, d), F32),
            jax.ShapeDtypeStruct((m, d), BF16),
            jax.ShapeDtypeStruct((m // HALF, d), F32),
            jax.ShapeDtypeStruct((m, GLA_KEY), F32),
            jax.ShapeDtypeStruct((m // HALF, GLA_KEY), F32),
        ],
        scratch_shapes=[pltpu.VMEM((tm, d), BF16)],
        compiler_params=pltpu.CompilerParams(
            dimension_semantics=("parallel", "arbitrary"),
            vmem_limit_bytes=INPROJ_VMEM_LIMIT),
        name="inproj",
    )(xf, mod3, nw, w_main, w_hf, lb, wga, aw, ab)


def _fast_masks(group):
    shape = (HALF, group * CHUNK)
    row = lax.broadcasted_iota(jnp.int32, shape, 0)
    col = lax.broadcasted_iota(jnp.int32, shape, 1) & (CHUNK - 1)
    top = col <= row
    left = col < HALF
    bot = (col >= HALF) & (col <= row + HALF)
    return top, left, bot


def _block_diag(blocks):
    n = len(blocks)
    return jnp.concatenate(
        [jnp.concatenate([b if j == i else jnp.zeros_like(b) for j in range(n)], axis=1)
         for i, b in enumerate(blocks)], axis=0)


def _block_cumsum8(g):
    g3 = g.reshape(NBLK, SUBLANES, g.shape[-1])
    sub = lax.broadcasted_iota(jnp.int32, g3.shape, 1)
    for sh in (1, 2, 4):
        g3 = g3 + jnp.where(sub >= sh, pltpu.roll(g3, sh, axis=1), 0.0)
    return g3


_NT = (((1,), (1,)), ((), ()))
_TN = (((0,), (0,)), ((), ()))


def _gla_chunk_fast(q, k, vb, g, sts, masks):
    top_mask, left_mask, bot_mask = masks
    n = len(sts)
    vdim = vb.shape[1] // n
    kcol = lambda h: slice(h * HEAD_K, (h + 1) * HEAD_K)
    vcol = lambda h: slice(h * vdim, (h + 1) * vdim)
    c8a = _block_cumsum8(g)
    c8 = [c8a[i] for i in range(NBLK)]
    t8 = [ci[SUBLANES - 1:SUBLANES, :] for ci in c8]
    per = HALF // SUBLANES
    c32, t32 = [], []
    for half in range(2):
        run = None
        for i in range(half * per, (half + 1) * per):
            c32.append(c8[i] if run is None else c8[i] + run)
            run = t8[i] if run is None else run + t8[i]
        t32.append(run)
    c32 = jnp.concatenate(c32, axis=0)
    e0 = jnp.exp(t32[0])
    e1 = jnp.exp(t32[1])
    e01 = e0 * e1
    q32 = q * jnp.exp(c32)
    kd = k * jnp.exp(-c32)
    q32b = q32.astype(BF16)
    q64_hi = (q32[HALF:] * e0).astype(BF16)
    lhs = jnp.concatenate([q32b, q64_hi], axis=0)
    kdb = kd.astype(BF16)
    sc = lax.dot_general(lhs, _block_diag([kdb[:, kcol(h)] for h in range(n)]), _NT,
                         preferred_element_type=F32)
    top = jnp.where(top_mask, sc[:HALF], 0.0)
    bot = jnp.where(left_mask, sc[CHUNK:], jnp.where(bot_mask, sc[HALF:CHUNK], 0.0))
    scores = jnp.concatenate([top, bot], axis=0).astype(BF16)

    q64 = jnp.concatenate([q32b[:HALF], q64_hi], axis=0)
    o = jnp.dot(scores, _block_diag([vb[:, vcol(h)] for h in range(n)]),
                preferred_element_type=F32)
    o = o + lax.dot_general(q64, _block_diag([st.astype(BF16) for st in sts]), _NT,
                            preferred_element_type=F32)
    k64 = jnp.concatenate([kd[:HALF] * e01, kd[HALF:] * e1], axis=0).astype(BF16)
    new_sts = [sts[h] * e01[:, kcol(h)]
               + lax.dot_general(vb[:, vcol(h)], k64[:, kcol(h)], _TN, preferred_element_type=F32)
               for h in range(n)]
    return o, new_sts


def _exact_masks():
    row = lax.broadcasted_iota(jnp.int32, (CHUNK, CHUNK), 0)
    col = lax.broadcasted_iota(jnp.int32, (CHUNK, CHUNK), 1)
    row_in = row & (SUBLANES - 1)
    row_base = row - row_in
    diag = [(col == row_base + s) & (row_in >= s) for s in range(SUBLANES)]
    levels = []
    for m in OFFDIAG_LEVELS:
        sh = m.bit_length() - 1
        tb = row >> sh
        sb = col >> sh
        levels.append(((tb & 1) == 1) & (sb == tb - 1))
    return diag, levels


def _gla_chunk_exact(q, k, vb, g, st, masks):
    diag_masks, level_masks = masks
    kdim = q.shape[-1]
    c8a = _block_cumsum8(g)

    c = {8: [c8a[i] for i in range(NBLK)]}
    t = {8: [ci[SUBLANES - 1:SUBLANES, :] for ci in c[8]]}
    m = 8
    while m < CHUNK:
        per = m // SUBLANES
        nxt = []
        for i in range(NBLK):
            mb = i // per
            nxt.append(c[m][i] + t[m][mb - 1] if mb % 2 == 1 else c[m][i])
        c[2 * m] = nxt
        t[2 * m] = [t[m][2 * j] + t[m][2 * j + 1] for j in range(len(t[m]) // 2)]
        m *= 2

    def cat(blocks):
        return jnp.concatenate(blocks, axis=0)

    def q_scaled(m):
        return (q * jnp.exp(cat(c[m]))).astype(BF16)

    def k_scaled(m):
        per = m // SUBLANES
        rest = [t[m][i // per] - c[m][i] for i in range(NBLK)]
        return (k * jnp.exp(cat(rest))).astype(BF16)

    q3 = q.reshape(NBLK, SUBLANES, kdim)
    k3 = k.reshape(NBLK, SUBLANES, kdim)
    scores = jnp.zeros((CHUNK, CHUNK), F32)
    for s in range(SUBLANES):
        e = jnp.exp(jnp.minimum(c8a - c8a[:, s:s + 1, :], 0.0))
        p = q3 * e * k3[:, s:s + 1, :]
        r = jnp.sum(p, axis=-1, keepdims=True).reshape(CHUNK, 1)
        scores = scores + jnp.where(diag_masks[s], r, 0.0)
    for m, mask in zip(OFFDIAG_LEVELS, level_masks):
        sc = lax.dot_general(q_scaled(m), k_scaled(m), _NT, preferred_element_type=F32)
        scores = scores + jnp.where(mask, sc, 0.0)

    o = jnp.dot(scores.astype(BF16), vb, preferred_element_type=F32)
    o = o + lax.dot_general(q_scaled(CHUNK), st.astype(BF16), _NT, preferred_element_type=F32)
    upd = lax.dot_general(vb, k_scaled(CHUNK), _TN, preferred_element_type=F32)
    st_new = st * jnp.exp(t[CHUNK][0]) + upd
    return o, st_new


def _scan_step(load_qkg, v_ref, gsum_ref, o_ref, st_ref, *, tb, heads, head_v, group):
    @pl.when(pl.program_id(1) == 0)
    def _():
        st_ref[...] = jnp.zeros_like(st_ref)

    safe = jnp.min(gsum_ref[...]) >= -SAFE_DECAY

    def head_chunk(chunk_fn, rows, h, masks):
        kc = slice(h * HEAD_K, (h + 1) * HEAD_K)
        vc = slice(h * head_v, (h + 1) * head_v)
        q, k, g = load_qkg(rows, kc)
        o, st_new = chunk_fn(q, k, v_ref[rows, vc], g, st_ref[h], masks)
        st_ref[h] = st_new
        o_ref[rows, vc] = o.astype(o_ref.dtype)

    def head_group_chunk(rows, h0, masks):
        hs = range(h0, h0 + group)
        kc = slice(h0 * HEAD_K, (h0 + group) * HEAD_K)
        vc = slice(h0 * head_v, (h0 + group) * head_v)
        q, k, g = load_qkg(rows, kc)
        o, new_sts = _gla_chunk_fast(q, k, v_ref[rows, vc], g, [st_ref[h] for h in hs], masks)
        for h, st_new in zip(hs, new_sts):
            st_ref[h] = st_new
        o_ref[rows, vc] = o.astype(o_ref.dtype)

    @pl.when(safe)
    def _():
        masks = _fast_masks(group)
        for c in range(tb // CHUNK):
            for h0 in range(0, heads, group):
                head_group_chunk(pl.ds(c * CHUNK, CHUNK), h0, masks)

    @pl.when(jnp.logical_not(safe))
    def _():
        masks = _exact_masks()

        def body(ci, carry):
            rows = pl.ds(pl.multiple_of(ci * CHUNK, CHUNK), CHUNK)
            for h in range(heads):
                head_chunk(_gla_chunk_exact, rows, h, masks)
            return carry

        lax.fori_loop(0, tb // CHUNK, body, 0)


def _hgrn_scan_kernel(q_ref, k_ref, v_ref, g_ref, gsum_ref, o_ref, st_ref, *, tb):
    def load_qkg(rows, kc):
        return q_ref[rows, kc].astype(F32), k_ref[rows, kc].astype(F32), g_ref[rows, kc]

    _scan_step(load_qkg, v_ref, gsum_ref, o_ref, st_ref, tb=tb, heads=HGRN_HEADS,
               head_v=HGRN_HEAD_V, group=HGRN_HEAD_GROUP)


def _gla_scan_kernel(qk_ref, v_ref, g_ref, gsum_ref, o_ref, st_ref, *, tb):
    scale = HEAD_K ** -0.5

    def load_qkg(rows, kc):
        kc2 = slice(GLA_KEY + kc.start, GLA_KEY + kc.stop)
        return (qk_ref[rows, kc].astype(F32) * scale, qk_ref[rows, kc2].astype(F32),
                g_ref[rows, kc])

    _scan_step(load_qkg, v_ref, gsum_ref, o_ref, st_ref, tb=tb, heads=GLA_HEADS,
               head_v=GLA_HEAD_V, group=GLA_HEAD_GROUP)


def _scan_call(kernel_fn, operands, in_specs, *, name, m, tb, batch, nt, heads, head_v):
    d = D_MODEL
    return pl.pallas_call(
        functools.partial(kernel_fn, tb=tb),
        grid=(batch, nt),
        in_specs=in_specs,
        out_specs=pl.BlockSpec((tb, d), lambda b, t: (b * nt + t, 0)),
        out_shape=jax.ShapeDtypeStruct((m, d), BF16),
        scratch_shapes=[pltpu.VMEM((heads, head_v, HEAD_K), F32)],
        compiler_params=pltpu.CompilerParams(dimension_semantics=("parallel", "arbitrary"),
                                             vmem_limit_bytes=SCAN_VMEM_LIMIT),
        name=name,
    )(*operands)


def _hgrn_scan(proj, k, g, gsum, *, batch, seq, tb=1024):
    nt = seq // tb
    blk = lambda r, c, col: pl.BlockSpec((r, c), lambda b, t: (b * nt + t, col))
    in_specs = [blk(tb, PROJ_BLOCK, BLK_HQ), blk(tb, PROJ_BLOCK, 0), blk(tb, PROJ_BLOCK, BLK_HI),
                blk(tb, PROJ_BLOCK, 0), blk(tb // HALF, PROJ_BLOCK, 0)]
    return _scan_call(_hgrn_scan_kernel, (proj, k, proj, g, gsum), in_specs, name="hgrn_scan",
                      m=proj.shape[0], tb=tb, batch=batch, nt=nt, heads=HGRN_HEADS,
                      head_v=HGRN_HEAD_V)


def _gla_scan(proj, loga, lsum, *, batch, seq, tb=1024):
    nt = seq // tb
    blk = lambda r, c, col: pl.BlockSpec((r, c), lambda b, t: (b * nt + t, col))
    in_specs = [blk(tb, PROJ_BLOCK, BLK_GQK), blk(tb, PROJ_BLOCK, BLK_GV),
                blk(tb, GLA_KEY, 0), blk(tb // HALF, GLA_KEY, 0)]
    return _scan_call(_gla_scan_kernel, (proj, proj, loga, lsum), in_specs, name="gla_scan",
                      m=proj.shape[0], tb=tb, batch=batch, nt=nt, heads=GLA_HEADS,
                      head_v=GLA_HEAD_V)


def _rms(x, w):
    return x * lax.rsqrt(jnp.mean(x * x, axis=-1, keepdims=True) + EPS) * w


def _out_kernel(oh_ref, og_ref, hz_ref, gz_ref, mg0_ref, mg1_ref, x_ref, mod_ref,
                hnw_ref, gnw_ref, wb0_ref, wb1_ref, wo_ref, fnw_ref, out_ref, *, final):
    f32 = lambda ref: ref[...].astype(F32)
    yh = _rms(f32(oh_ref), hnw_ref[...]) * f32(hz_ref)
    gnw = gnw_ref[...]
    og = jnp.concatenate(
        [_rms(og_ref[:, h * GLA_HEAD_V:(h + 1) * GLA_HEAD_V].astype(F32), gnw)
         for h in range(GLA_HEADS)], axis=-1)
    yg = og * f32(gz_ref)
    uh = jnp.dot(yh.astype(BF16), wb0_ref[...], preferred_element_type=F32)
    ug = jnp.dot(yg.astype(BF16), wb1_ref[...], preferred_element_type=F32)
    merged = f32(mg0_ref) * uh + f32(mg1_ref) * ug
    gate = mod_ref[0, 2:3, :]
    out = x_ref[...] + gate * jnp.dot(merged.astype(BF16), wo_ref[...],
                                      preferred_element_type=F32)
    if final:
        out = _rms(out, fnw_ref[...])
    out_ref[...] = out


def _outproj(oh, og, proj, xf, mod3, hnw, gnw, wb0, wb1, wo, fnw, *, seq, final, tm=512):
    m, d = xf.shape
    per_batch = seq // tm
    rows = lambda c: pl.BlockSpec((tm, PROJ_BLOCK), lambda i: (i, c))
    const = lambda shape: pl.BlockSpec(shape, lambda i: tuple(0 for _ in shape))
    return pl.pallas_call(
        functools.partial(_out_kernel, final=final),
        grid=(m // tm,),
        in_specs=[
            rows(0), rows(0), rows(BLK_HZ), rows(BLK_GZ), rows(BLK_MG0), rows(BLK_MG1), rows(0),
            pl.BlockSpec((1, 3, d), lambda i: (i // per_batch, 0, 0)),
            const((1, d)), const((1, GLA_HEAD_V)),
            const((d, d)), const((d, d)), const((d, d)), const((1, d)),
        ],
        out_specs=rows(0),
        out_shape=jax.ShapeDtypeStruct((m, d), F32),
        compiler_params=pltpu.CompilerParams(
            dimension_semantics=("parallel",), vmem_limit_bytes=SCAN_VMEM_LIMIT),
        name="outproj",
    )(oh, og, proj, proj, proj, proj, xf, mod3, hnw, gnw, wb0, wb1, wo, fnw)


def kernel(x, c, ada_w, ada_b, norm_w, w_in, hgrn_lb_logits, hgrn_norm_w, gla_alpha_w,
           gla_alpha_b, gla_norm_w, w_branch, w_out, final_norm_w):
    batch, seq, d = x.shape
    depth = ada_w.shape[0]
    mod_all, lb_all = _prep(c, ada_w, ada_b, hgrn_lb_logits)
    xf = x.reshape(batch * seq, d)
    ga_hi = COL_GA + GLA_GATE_RANK
    for l in range(depth):
        w_main = jnp.concatenate([w_in[l][:, :COL_HF], w_in[l][:, COL_HI:COL_GA], w_in[l][:, ga_hi:]],
                                 axis=1).astype(BF16)
        w_hf = w_in[l][:, COL_HF:COL_HI].astype(BF16)
        wga = jnp.pad(w_in[l][:, COL_GA:ga_hi], ((0, 0), (0, LANES - GLA_GATE_RANK))).astype(BF16)
        aw = jnp.pad(gla_alpha_w[l], ((0, LANES - GLA_GATE_RANK), (0, 0))).astype(BF16)
        mod3 = mod_all[l].reshape(batch, 3, d)
        proj, g, k, gsum, loga, lsum = _inproj(xf, mod3, norm_w[l][None], w_main, w_hf,
                                               lb_all[l][None], wga, aw, gla_alpha_b[l][None],
                                               seq=seq)
        oh = _hgrn_scan(proj, k, g, gsum, batch=batch, seq=seq)
        og = _gla_scan(proj, loga, lsum, batch=batch, seq=seq)
        xf = _outproj(oh, og, proj, xf, mod3, hgrn_norm_w[l][None], gla_norm_w[l][None],
                      w_branch[l, 0].astype(BF16), w_branch[l, 1].astype(BF16),
                      w_out[l].astype(BF16), final_norm_w[None],
                      seq=seq, final=(l == depth - 1))
    return xf.reshape(batch, seq, d)
```

```python
import functools

import jax
import jax.numpy as jnp
from jax import lax
from jax.experimental import pallas as pl
from jax.experimental.pallas import tpu as pltpu

F32 = jnp.float32
BF16 = jnp.bfloat16

D_MODEL = 1024
DEPTH = 2
HGRN_HEADS = 8
GLA_HEADS = 4
HEAD_K = 128
HGRN_HEAD_V = 128
GLA_HEAD_V = 256
GLA_KEY = GLA_HEADS * HEAD_K
GLA_GATE_RANK = 16
GLA_GATE_TAU = 16.0
EPS = 1e-6
F_FLOOR = 1e-20

LANES = 128
SUBLANES = 8
CHUNK = 64
NBLK = CHUNK // SUBLANES
HALF = CHUNK // 2
HGRN_HEAD_GROUP = 2
GLA_HEAD_GROUP = 1
SAFE_DECAY = 70.0
OFFDIAG_LEVELS = (8, 16, 32)

COL_GA = 7168
PROJ_BLOCK = 1024
BLK_HQ, BLK_HI, BLK_HZ, BLK_GQK, BLK_GV, BLK_GZ, BLK_MG0, BLK_MG1 = range(8)
ACT_ID, ACT_SILU, ACT_SIGMOID = range(3)
BLOCK_ACT = (ACT_SILU, ACT_ID, ACT_SILU, ACT_ID, ACT_ID, ACT_SILU, ACT_SIGMOID, ACT_SIGMOID)
MG_SRC = 100
CAST_SRC_BLOCKS = (0, 2, 3, 4, 5, 6, MG_SRC + 0, MG_SRC + 1, 1)
CAST_BLK_HF = 8
SCAN_VMEM_LIMIT = 48 * 1024 * 1024
INPROJ_VMEM_LIMIT = 56 * 1024 * 1024


def _silu(x):
    return x * jax.nn.sigmoid(x)


def _mod_kernel(c_ref, w_ref, b_ref, mod_ref):
    c = c_ref[...]
    c_act = _silu(c)
    mod_ref[0] = jnp.dot(c_act, w_ref[0], preferred_element_type=F32) + b_ref[0]


def _lb_kernel(logit_ref, lb_ref):
    x = logit_ref[...]
    m = jnp.max(x, axis=0, keepdims=True)
    e = jnp.exp(x - m)
    p = e / jnp.sum(e, axis=0, keepdims=True)
    run = jnp.zeros_like(p[0:1])
    for l in range(x.shape[0]):
        run = run + p[l:l + 1]
        lb_ref[l:l + 1, :] = run - p[0:1]


def _prep(c, ada_w, ada_b, lb_logits):
    depth, d, d3 = ada_w.shape
    b = c.shape[0]
    nblk = d3 // d
    mod = pl.pallas_call(
        _mod_kernel,
        grid=(depth, nblk),
        in_specs=[
            pl.BlockSpec((b, d), lambda l, n: (0, 0)),
            pl.BlockSpec((1, d, d), lambda l, n: (l, 0, n)),
            pl.BlockSpec((1, b, d), lambda l, n: (l, 0, n)),
        ],
        out_specs=pl.BlockSpec((1, b, d), lambda l, n: (l, 0, n)),
        out_shape=jax.ShapeDtypeStruct((depth, b, d3), F32),
        name="adaln_mod",
    )(c, ada_w, jnp.broadcast_to(ada_b[:, None, :], (depth, b, d3)))
    lb = pl.pallas_call(
        _lb_kernel,
        out_shape=jax.ShapeDtypeStruct(lb_logits.shape, F32),
        name="hgrn_lower_bounds",
    )(lb_logits)
    return mod, lb


def _sigmoid(x):
    return 0.5 * jnp.tanh(0.5 * x) + 0.5


def _cast_kernel(w_ref, wmg_ref, o_ref):
    from_mg = [j for j, src in enumerate(CAST_SRC_BLOCKS) if src >= MG_SRC]
    is_mg = functools.reduce(jnp.logical_or, [pl.program_id(1) == j for j in from_mg])

    @pl.when(is_mg)
    def _():
        o_ref[...] = wmg_ref[...].astype(BF16)

    @pl.when(jnp.logical_not(is_mg))
    def _():
        o_ref[...] = w_ref[...].astype(BF16)


def _cast_in_weights(w_in, w_mg):
    depth, d, _ = w_in.shape
    nblk = len(CAST_SRC_BLOCKS)

    def pick(j, from_mg):
        return sum(jnp.where(j == out_blk, src % MG_SRC, 0)
                   for out_blk, src in enumerate(CAST_SRC_BLOCKS) if (src >= MG_SRC) == from_mg)

    return pl.pallas_call(
        _cast_kernel,
        grid=(depth, nblk),
        in_specs=[pl.BlockSpec((1, d, PROJ_BLOCK), lambda l, j: (l, 0, pick(j, False))),
                  pl.BlockSpec((1, d, PROJ_BLOCK), lambda l, j: (l, 0, pick(j, True)))],
        out_specs=pl.BlockSpec((1, d, PROJ_BLOCK), lambda l, j: (l, 0, j)),
        out_shape=jax.ShapeDtypeStruct((depth, d, nblk * PROJ_BLOCK), BF16),
        compiler_params=pltpu.CompilerParams(dimension_semantics=("parallel", "arbitrary")),
        name="cast_in_weights",
    )(w_in, w_mg)


def _half_block_sums(g):
    rows, cols = g.shape
    return jnp.sum(g.reshape(rows // HALF, HALF, cols), axis=1)


def _step_acts(per_step):
    return [BLOCK_ACT[s:s + per_step] for s in range(0, len(BLOCK_ACT), per_step)]


def _step_has_acts(j, acts, per_step):
    hits = [j == s for s, a in enumerate(_step_acts(per_step)) if a == acts]
    return functools.reduce(jnp.logical_or, hits)


def _inproj_kernel(x_ref, mod_ref, nw_ref, w_ref, whf_ref, lb_ref, wga_ref, aw_ref, ab_ref,
                   out_ref, g_ref, k_ref, gsum_ref, loga_ref, lsum_ref, h_ref):
    j = pl.program_id(1)

    @pl.when(j == 0)
    def _():
        x = x_ref[...]
        ms = jnp.mean(x * x, axis=-1, keepdims=True)
        y = x * lax.rsqrt(ms + EPS) * nw_ref[...]
        shift = mod_ref[0, 0:1, :]
        scale = mod_ref[0, 1:2, :]
        hb = (y * (1.0 + scale) + shift).astype(BF16)
        h_ref[...] = hb
        lb = jnp.clip(lb_ref[...], 0.0, 1.0)
        hf = jnp.dot(hb, whf_ref[0], preferred_element_type=F32)
        f = lb + (1.0 - lb) * _sigmoid(hf)
        g = jnp.log(jnp.maximum(f, F_FLOOR))
        g_ref[...] = g
        k_ref[...] = (1.0 - f).astype(BF16)
        gsum_ref[...] = _half_block_sums(g)
        ga = jnp.dot(hb, wga_ref[...], preferred_element_type=F32)
        z = jnp.dot(ga.astype(BF16), aw_ref[...], preferred_element_type=F32) + ab_ref[...]
        log_sig = jnp.minimum(z, 0.0) - jnp.log1p(jnp.exp(-jnp.abs(z)))
        loga = log_sig * (1.0 / GLA_GATE_TAU)
        loga_ref[...] = loga
        lsum_ref[...] = _half_block_sums(loga)

    per_step = out_ref.shape[1] // PROJ_BLOCK

    def project(acts):
        for s, act in enumerate(acts):
            cols = slice(s * PROJ_BLOCK, (s + 1) * PROJ_BLOCK)
            acc = jnp.dot(h_ref[...], w_ref[0, :, cols], preferred_element_type=F32)
            if act != ACT_ID:
                sig = _sigmoid(acc)
                acc = acc * sig if act == ACT_SILU else sig
            out_ref[:, cols] = acc.astype(BF16)

    for acts in sorted(set(_step_acts(per_step))):
        pl.when(_step_has_acts(j, acts, per_step))(functools.partial(project, acts))


def _inproj(xf, mod3, nw, w_cast, lb, wga, aw, ab, *, layer, seq, tm=1024, tn=2 * PROJ_BLOCK):
    m, d = xf.shape
    n = len(BLOCK_ACT) * PROJ_BLOCK
    per_batch = seq // tm
    const = lambda shape: pl.BlockSpec(shape, lambda i, j: (0, 0))
    rows = lambda r, c: pl.BlockSpec((r, c), lambda i, j: (i, 0))
    return pl.pallas_call(
        _inproj_kernel,
        grid=(m // tm, n // tn),
        in_specs=[
            rows(tm, d),
            pl.BlockSpec((1, 3, d), lambda i, j: (i // per_batch, 0, 0)),
            const((1, d)),
            pl.BlockSpec((1, d, tn), lambda i, j: (layer, 0, j)),
            pl.BlockSpec((1, d, PROJ_BLOCK), lambda i, j: (layer, 0, CAST_BLK_HF)),
            const((1, d)),
            const((d, LANES)),
            const((LANES, GLA_KEY)),
            const((1, GLA_KEY)),
        ],
        out_specs=[
            pl.BlockSpec((tm, tn), lambda i, j: (i, j)),
            rows(tm, d), rows(tm, d), rows(tm // HALF, d),
            rows(tm, GLA_KEY), rows(tm // HALF, GLA_KEY),
        ],
        out_shape=[
            jax.ShapeDtypeStruct((m, n), BF16),
            jax.ShapeDtypeStruct((m, d), F32),
            jax.ShapeDtypeStruct((m, d), BF16),
            jax.ShapeDtypeStruct((m // HALF, d), F32),
            jax.ShapeDtypeStruct((m, GLA_KEY), F32),
            jax.ShapeDtypeStruct((m // HALF, GLA_KEY), F32),
        ],
        scratch_shapes=[pltpu.VMEM((tm, d), BF16)],
        compiler_params=pltpu.CompilerParams(
            dimension_semantics=("parallel", "arbitrary"),
            vmem_limit_bytes=INPROJ_VMEM_LIMIT),
        name="inproj",
    )(xf, mod3, nw, w_cast, w_cast, lb, wga, aw, ab)


def _fast_masks(group):
    shape = (HALF, group * CHUNK)
    row = lax.broadcasted_iota(jnp.int32, shape, 0)
    col = lax.broadcasted_iota(jnp.int32, shape, 1) & (CHUNK - 1)
    top = col <= row
    left = col < HALF
    bot = (col >= HALF) & (col <= row + HALF)
    return top, left, bot


def _block_diag(blocks):
    n = len(blocks)
    return jnp.concatenate(
        [jnp.concatenate([b if j == i else jnp.zeros_like(b) for j in range(n)], axis=1)
         for i, b in enumerate(blocks)], axis=0)


def _block_cumsum8(g):
    g3 = g.reshape(NBLK, SUBLANES, g.shape[-1])
    sub = lax.broadcasted_iota(jnp.int32, g3.shape, 1)
    for sh in (1, 2, 4):
        g3 = g3 + jnp.where(sub >= sh, pltpu.roll(g3, sh, axis=1), 0.0)
    return g3


_NT = (((1,), (1,)), ((), ()))
_TN = (((0,), (0,)), ((), ()))


def _gla_chunk_fast(q, k, vb, g, sts, masks):
    top_mask, left_mask, bot_mask = masks
    n = len(sts)
    vdim = vb.shape[1] // n
    kcol = lambda h: slice(h * HEAD_K, (h + 1) * HEAD_K)
    vcol = lambda h: slice(h * vdim, (h + 1) * vdim)
    c8a = _block_cumsum8(g)
    c8 = [c8a[i] for i in range(NBLK)]
    t8 = [ci[SUBLANES - 1:SUBLANES, :] for ci in c8]
    per = HALF // SUBLANES
    c32, t32 = [], []
    for half in range(2):
        run = None
        for i in range(half * per, (half + 1) * per):
            c32.append(c8[i] if run is None else c8[i] + run)
            run = t8[i] if run is None else run + t8[i]
        t32.append(run)
    c32 = jnp.concatenate(c32, axis=0)
    e0 = jnp.exp(t32[0])
    e1 = jnp.exp(t32[1])
    e01 = e0 * e1
    q32 = q * jnp.exp(c32)
    kd = k * jnp.exp(-c32)
    q32b = q32.astype(BF16)
    q64_hi = (q32[HALF:] * e0).astype(BF16)
    lhs = jnp.concatenate([q32b, q64_hi], axis=0)
    kdb = kd.astype(BF16)
    sc = lax.dot_general(lhs, _block_diag([kdb[:, kcol(h)] for h in range(n)]), _NT,
                         preferred_element_type=F32)
    top = jnp.where(top_mask, sc[:HALF], 0.0)
    bot = jnp.where(left_mask, sc[CHUNK:], jnp.where(bot_mask, sc[HALF:CHUNK], 0.0))
    scores = jnp.concatenate([top, bot], axis=0).astype(BF16)

    q64 = jnp.concatenate([q32b[:HALF], q64_hi], axis=0)
    o = jnp.dot(scores, _block_diag([vb[:, vcol(h)] for h in range(n)]),
                preferred_element_type=F32)
    o = o + lax.dot_general(q64, _block_diag([st.astype(BF16) for st in sts]), _NT,
                            preferred_element_type=F32)
    k64 = jnp.concatenate([kd[:HALF] * e01, kd[HALF:] * e1], axis=0).astype(BF16)
    new_sts = [sts[h] * e01[:, kcol(h)]
               + lax.dot_general(vb[:, vcol(h)], k64[:, kcol(h)], _TN, preferred_element_type=F32)
               for h in range(n)]
    return o, new_sts


def _exact_masks():
    row = lax.broadcasted_iota(jnp.int32, (CHUNK, CHUNK), 0)
    col = lax.broadcasted_iota(jnp.int32, (CHUNK, CHUNK), 1)
    row_in = row & (SUBLANES - 1)
    row_base = row - row_in
    diag = [(col == row_base + s) & (row_in >= s) for s in range(SUBLANES)]
    levels = []
    for m in OFFDIAG_LEVELS:
        sh = m.bit_length() - 1
        tb = row >> sh
        sb = col >> sh
        levels.append(((tb & 1) == 1) & (sb == tb - 1))
    return diag, levels


def _gla_chunk_exact(q, k, vb, g, st, masks):
    diag_masks, level_masks = masks
    kdim = q.shape[-1]
    c8a = _block_cumsum8(g)

    c = {8: [c8a[i] for i in range(NBLK)]}
    t = {8: [ci[SUBLANES - 1:SUBLANES, :] for ci in c[8]]}
    m = 8
    while m < CHUNK:
        per = m // SUBLANES
        nxt = []
        for i in range(NBLK):
            mb = i // per
            nxt.append(c[m][i] + t[m][mb - 1] if mb % 2 == 1 else c[m][i])
        c[2 * m] = nxt
        t[2 * m] = [t[m][2 * j] + t[m][2 * j + 1] for j in range(len(t[m]) // 2)]
        m *= 2

    def cat(blocks):
        return jnp.concatenate(blocks, axis=0)

    def q_scaled(m):
        return (q * jnp.exp(cat(c[m]))).astype(BF16)

    def k_scaled(m):
        per = m // SUBLANES
        rest = [t[m][i // per] - c[m][i] for i in range(NBLK)]
        return (k * jnp.exp(cat(rest))).astype(BF16)

    q3 = q.reshape(NBLK, SUBLANES, kdim)
    k3 = k.reshape(NBLK, SUBLANES, kdim)
    scores = jnp.zeros((CHUNK, CHUNK), F32)
    for s in range(SUBLANES):
        e = jnp.exp(jnp.minimum(c8a - c8a[:, s:s + 1, :], 0.0))
        p = q3 * e * k3[:, s:s + 1, :]
        r = jnp.sum(p, axis=-1, keepdims=True).reshape(CHUNK, 1)
        scores = scores + jnp.where(diag_masks[s], r, 0.0)
    for m, mask in zip(OFFDIAG_LEVELS, level_masks):
        sc = lax.dot_general(q_scaled(m), k_scaled(m), _NT, preferred_element_type=F32)
        scores = scores + jnp.where(mask, sc, 0.0)

    o = jnp.dot(scores.astype(BF16), vb, preferred_element_type=F32)
    o = o + lax.dot_general(q_scaled(CHUNK), st.astype(BF16), _NT, preferred_element_type=F32)
    upd = lax.dot_general(vb, k_scaled(CHUNK), _TN, preferred_element_type=F32)
    st_new = st * jnp.exp(t[CHUNK][0]) + upd
    return o, st_new


def _scan_step(load_qkg, v_ref, gsum_ref, o_ref, st_ref, *, tb, heads, head_v, group):
    @pl.when(pl.program_id(1) == 0)
    def _():
        st_ref[...] = jnp.zeros_like(st_ref)

    safe = jnp.min(gsum_ref[...]) >= -SAFE_DECAY

    def head_chunk(chunk_fn, rows, h, masks):
        kc = slice(h * HEAD_K, (h + 1) * HEAD_K)
        vc = slice(h * head_v, (h + 1) * head_v)
        q, k, g = load_qkg(rows, kc)
        o, st_new = chunk_fn(q, k, v_ref[rows, vc], g, st_ref[h], masks)
        st_ref[h] = st_new
        o_ref[rows, vc] = o.astype(o_ref.dtype)

    def head_group_chunk(rows, h0, masks):
        hs = range(h0, h0 + group)
        kc = slice(h0 * HEAD_K, (h0 + group) * HEAD_K)
        vc = slice(h0 * head_v, (h0 + group) * head_v)
        q, k, g = load_qkg(rows, kc)
        o, new_sts = _gla_chunk_fast(q, k, v_ref[rows, vc], g, [st_ref[h] for h in hs], masks)
        for h, st_new in zip(hs, new_sts):
            st_ref[h] = st_new
        o_ref[rows, vc] = o.astype(o_ref.dtype)

    @pl.when(safe)
    def _():
        masks = _fast_masks(group)
        for c in range(tb // CHUNK):
            for h0 in range(0, heads, group):
                head_group_chunk(pl.ds(c * CHUNK, CHUNK), h0, masks)

    @pl.when(jnp.logical_not(safe))
    def _():
        masks = _exact_masks()

        def body(ci, carry):
            rows = pl.ds(pl.multiple_of(ci * CHUNK, CHUNK), CHUNK)
            for h in range(heads):
                head_chunk(_gla_chunk_exact, rows, h, masks)
            return carry

        lax.fori_loop(0, tb // CHUNK, body, 0)


def _hgrn_scan_kernel(q_ref, k_ref, v_ref, g_ref, gsum_ref, o_ref, st_ref, *, tb):
    def load_qkg(rows, kc):
        return q_ref[rows, kc].astype(F32), k_ref[rows, kc].astype(F32), g_ref[rows, kc]

    _scan_step(load_qkg, v_ref, gsum_ref, o_ref, st_ref, tb=tb, heads=HGRN_HEADS,
               head_v=HGRN_HEAD_V, group=HGRN_HEAD_GROUP)


def _gla_scan_kernel(qk_ref, v_ref, g_ref, gsum_ref, o_ref, st_ref, *, tb):
    scale = HEAD_K ** -0.5

    def load_qkg(rows, kc):
        kc2 = slice(GLA_KEY + kc.start, GLA_KEY + kc.stop)
        return (qk_ref[rows, kc].astype(F32) * scale, qk_ref[rows, kc2].astype(F32),
                g_ref[rows, kc])

    _scan_step(load_qkg, v_ref, gsum_ref, o_ref, st_ref, tb=tb, heads=GLA_HEADS,
               head_v=GLA_HEAD_V, group=GLA_HEAD_GROUP)


def _scan_call(kernel_fn, operands, in_specs, *, name, m, tb, batch, nt, heads, head_v):
    d = D_MODEL
    return pl.pallas_call(
        functools.partial(kernel_fn, tb=tb),
        grid=(batch, nt),
        in_specs=in_specs,
        out_specs=pl.BlockSpec((tb, d), lambda b, t: (b * nt + t, 0)),
        out_shape=jax.ShapeDtypeStruct((m, d), BF16),
        scratch_shapes=[pltpu.VMEM((heads, head_v, HEAD_K), F32)],
        compiler_params=pltpu.CompilerParams(dimension_semantics=("parallel", "arbitrary"),
                                             vmem_limit_bytes=SCAN_VMEM_LIMIT),
        name=name,
    )(*operands)


def _hgrn_scan(proj, k, g, gsum, *, batch, seq, tb=1024):
    nt = seq // tb
    blk = lambda r, c, col: pl.BlockSpec((r, c), lambda b, t: (b * nt + t, col))
    in_specs = [blk(tb, PROJ_BLOCK, BLK_HQ), blk(tb, PROJ_BLOCK, 0), blk(tb, PROJ_BLOCK, BLK_HI),
                blk(tb, PROJ_BLOCK, 0), blk(tb // HALF, PROJ_BLOCK, 0)]
    return _scan_call(_hgrn_scan_kernel, (proj, k, proj, g, gsum), in_specs, name="hgrn_scan",
                      m=proj.shape[0], tb=tb, batch=batch, nt=nt, heads=HGRN_HEADS,
                      head_v=HGRN_HEAD_V)


def _gla_scan(proj, loga, lsum, *, batch, seq, tb=1024):
    nt = seq // tb
    blk = lambda r, c, col: pl.BlockSpec((r, c), lambda b, t: (b * nt + t, col))
    in_specs = [blk(tb, PROJ_BLOCK, BLK_GQK), blk(tb, PROJ_BLOCK, BLK_GV),
                blk(tb, GLA_KEY, 0), blk(tb // HALF, GLA_KEY, 0)]
    return _scan_call(_gla_scan_kernel, (proj, proj, loga, lsum), in_specs, name="gla_scan",
                      m=proj.shape[0], tb=tb, batch=batch, nt=nt, heads=GLA_HEADS,
                      head_v=GLA_HEAD_V)


def _rms(x, w):
    return x * lax.rsqrt(jnp.mean(x * x, axis=-1, keepdims=True) + EPS) * w


def _out_kernel(oh_ref, og_ref, hz_ref, gz_ref, mg0_ref, mg1_ref, x_ref, mod_ref,
                hnw_ref, gnw_ref, wb0_ref, wb1_ref, wo_ref, fnw_ref, out_ref, *, final):
    f32 = lambda ref: ref[...].astype(F32)
    yh = _rms(f32(oh_ref), hnw_ref[...]) * f32(hz_ref)
    gnw = gnw_ref[...]
    og = jnp.concatenate(
        [_rms(og_ref[:, h * GLA_HEAD_V:(h + 1) * GLA_HEAD_V].astype(F32), gnw)
         for h in range(GLA_HEADS)], axis=-1)
    yg = og * f32(gz_ref)
    uh = jnp.dot(yh.astype(BF16), wb0_ref[...], preferred_element_type=F32)
    ug = jnp.dot(yg.astype(BF16), wb1_ref[...], preferred_element_type=F32)
    merged = f32(mg0_ref) * uh + f32(mg1_ref) * ug
    gate = mod_ref[0, 2:3, :]
    out = x_ref[...] + gate * jnp.dot(merged.astype(BF16), wo_ref[...],
                                      preferred_element_type=F32)
    if final:
        out = _rms(out, fnw_ref[...])
    out_ref[...] = out


def _outproj(oh, og, proj, xf, mod3, hnw, gnw, wb0, wb1, wo, fnw, *, seq, final, tm=512):
    m, d = xf.shape
    per_batch = seq // tm
    rows = lambda c: pl.BlockSpec((tm, PROJ_BLOCK), lambda i: (i, c))
    const = lambda shape: pl.BlockSpec(shape, lambda i: tuple(0 for _ in shape))
    return pl.pallas_call(
        functools.partial(_out_kernel, final=final),
        grid=(m // tm,),
        in_specs=[
            rows(0), rows(0), rows(BLK_HZ), rows(BLK_GZ), rows(BLK_MG0), rows(BLK_MG1), rows(0),
            pl.BlockSpec((1, 3, d), lambda i: (i // per_batch, 0, 0)),
            const((1, d)), const((1, GLA_HEAD_V)),
            const((d, d)), const((d, d)), const((d, d)), const((1, d)),
        ],
        out_specs=rows(0),
        out_shape=jax.ShapeDtypeStruct((m, d), F32),
        compiler_params=pltpu.CompilerParams(
            dimension_semantics=("parallel",), vmem_limit_bytes=SCAN_VMEM_LIMIT),
        name="outproj",
    )(oh, og, proj, proj, proj, proj, xf, mod3, hnw, gnw, wb0, wb1, wo, fnw)


def kernel(x, c, ada_w, ada_b, norm_w, w_in, hgrn_lb_logits, hgrn_norm_w, gla_alpha_w,
           gla_alpha_b, gla_norm_w, w_branch, w_out, final_norm_w):
    batch, seq, d = x.shape
    depth = ada_w.shape[0]
    mod_all, lb_all = _prep(c, ada_w, ada_b, hgrn_lb_logits)
    xf = x.reshape(batch * seq, d)
    ga_hi = COL_GA + GLA_GATE_RANK
    w_cast = _cast_in_weights(w_in, w_in[:, :, ga_hi:])
    for l in range(depth):
        wga = jnp.pad(w_in[l][:, COL_GA:ga_hi], ((0, 0), (0, LANES - GLA_GATE_RANK))).astype(BF16)
        aw = jnp.pad(gla_alpha_w[l], ((0, LANES - GLA_GATE_RANK), (0, 0))).astype(BF16)
        mod3 = mod_all[l].reshape(batch, 3, d)
        proj, g, k, gsum, loga, lsum = _inproj(xf, mod3, norm_w[l][None], w_cast, lb_all[l][None],
                                               wga, aw, gla_alpha_b[l][None], layer=l, seq=seq)
        oh = _hgrn_scan(proj, k, g, gsum, batch=batch, seq=seq)
        og = _gla_scan(proj, loga, lsum, batch=batch, seq=seq)
        xf = _outproj(oh, og, proj, xf, mod3, hgrn_norm_w[l][None], gla_norm_w[l][None],
                      w_branch[l, 0].astype(BF16), w_branch[l, 1].astype(BF16),
                      w_out[l].astype(BF16), final_norm_w[None],
                      seq=seq, final=(l == depth - 1))
    return xf.reshape(batch, seq, d)
```

```python
import functools

import jax
import jax.numpy as jnp
from jax import lax
from jax.experimental import pallas as pl
from jax.experimental.pallas import tpu as pltpu

F32 = jnp.float32
BF16 = jnp.bfloat16

D_MODEL = 1024
DEPTH = 2
HGRN_HEADS = 8
GLA_HEADS = 4
HEAD_K = 128
HGRN_HEAD_V = 128
GLA_HEAD_V = 256
GLA_KEY = GLA_HEADS * HEAD_K
GLA_GATE_RANK = 16
GLA_GATE_TAU = 16.0
EPS = 1e-6
F_FLOOR = 1e-20

LANES = 128
SUBLANES = 8
CHUNK = 64
NBLK = CHUNK // SUBLANES
HALF = CHUNK // 2
HGRN_HEAD_GROUP = 2
GLA_HEAD_GROUP = 1
SAFE_DECAY = 70.0
OFFDIAG_LEVELS = (8, 16, 32)

COL_HF = 1024
COL_HI = 2048
COL_GA = 7168
PROJ_BLOCK = 1024
BLK_HQ, BLK_HI, BLK_HZ, BLK_GQK, BLK_GV, BLK_GZ, BLK_MG0, BLK_MG1 = range(8)
ACT_ID, ACT_SILU, ACT_SIGMOID = range(3)
BLOCK_ACT = (ACT_SILU, ACT_ID, ACT_SILU, ACT_ID, ACT_ID, ACT_SILU, ACT_SIGMOID, ACT_SIGMOID)
SCAN_VMEM_LIMIT = 48 * 1024 * 1024
INPROJ_VMEM_LIMIT = 56 * 1024 * 1024


def _silu(x):
    return x * jax.nn.sigmoid(x)


def _mod_kernel(c_ref, w_ref, b_ref, mod_ref):
    c = c_ref[...]
    c_act = _silu(c)
    mod_ref[0] = jnp.dot(c_act, w_ref[0], preferred_element_type=F32) + b_ref[0]


def _lb_kernel(logit_ref, lb_ref):
    x = logit_ref[...]
    m = jnp.max(x, axis=0, keepdims=True)
    e = jnp.exp(x - m)
    p = e / jnp.sum(e, axis=0, keepdims=True)
    run = jnp.zeros_like(p[0:1])
    for l in range(x.shape[0]):
        run = run + p[l:l + 1]
        lb_ref[l:l + 1, :] = run - p[0:1]


def _prep(c, ada_w, ada_b, lb_logits):
    depth, d, d3 = ada_w.shape
    b = c.shape[0]
    nblk = d3 // d
    mod = pl.pallas_call(
        _mod_kernel,
        grid=(depth, nblk),
        in_specs=[
            pl.BlockSpec((b, d), lambda l, n: (0, 0)),
            pl.BlockSpec((1, d, d), lambda l, n: (l, 0, n)),
            pl.BlockSpec((1, b, d), lambda l, n: (l, 0, n)),
        ],
        out_specs=pl.BlockSpec((1, b, d), lambda l, n: (l, 0, n)),
        out_shape=jax.ShapeDtypeStruct((depth, b, d3), F32),
        name="adaln_mod",
    )(c, ada_w, jnp.broadcast_to(ada_b[:, None, :], (depth, b, d3)))
    lb = pl.pallas_call(
        _lb_kernel,
        out_shape=jax.ShapeDtypeStruct(lb_logits.shape, F32),
        name="hgrn_lower_bounds",
    )(lb_logits)
    return mod, lb


def _sigmoid(x):
    return 0.5 * jnp.tanh(0.5 * x) + 0.5


def _half_block_sums(g):
    rows, cols = g.shape
    return jnp.sum(g.reshape(rows // HALF, HALF, cols), axis=1)


def _step_acts(per_step):
    return [BLOCK_ACT[s:s + per_step] for s in range(0, len(BLOCK_ACT), per_step)]


def _step_has_acts(j, acts, per_step):
    hits = [j == s for s, a in enumerate(_step_acts(per_step)) if a == acts]
    return functools.reduce(jnp.logical_or, hits)


def _inproj_kernel(x_ref, mod_ref, nw_ref, w_ref, whf_ref, lb_ref, wga_ref, aw_ref, ab_ref,
                   out_ref, g_ref, k_ref, gsum_ref, loga_ref, lsum_ref, h_ref):
    j = pl.program_id(1)

    @pl.when(j == 0)
    def _():
        x = x_ref[...]
        ms = jnp.mean(x * x, axis=-1, keepdims=True)
        y = x * lax.rsqrt(ms + EPS) * nw_ref[...]
        shift = mod_ref[0, 0:1, :]
        scale = mod_ref[0, 1:2, :]
        hb = (y * (1.0 + scale) + shift).astype(BF16)
        h_ref[...] = hb
        lb = jnp.clip(lb_ref[...], 0.0, 1.0)
        hf = jnp.dot(hb, whf_ref[...], preferred_element_type=F32)
        f = lb + (1.0 - lb) * _sigmoid(hf)
        g = jnp.log(jnp.maximum(f, F_FLOOR))
        g_ref[...] = g
        k_ref[...] = (1.0 - f).astype(BF16)
        gsum_ref[...] = _half_block_sums(g)
        ga = jnp.dot(hb, wga_ref[...], preferred_element_type=F32)
        z = jnp.dot(ga.astype(BF16), aw_ref[...], preferred_element_type=F32) + ab_ref[...]
        log_sig = jnp.minimum(z, 0.0) - jnp.log1p(jnp.exp(-jnp.abs(z)))
        loga = log_sig * (1.0 / GLA_GATE_TAU)
        loga_ref[...] = loga
        lsum_ref[...] = _half_block_sums(loga)

    per_step = out_ref.shape[1] // PROJ_BLOCK

    def project(acts):
        for s, act in enumerate(acts):
            cols = slice(s * PROJ_BLOCK, (s + 1) * PROJ_BLOCK)
            acc = jnp.dot(h_ref[...], w_ref[:, cols], preferred_element_type=F32)
            if act != ACT_ID:
                sig = _sigmoid(acc)
                acc = acc * sig if act == ACT_SILU else sig
            out_ref[:, cols] = acc.astype(BF16)

    for acts in sorted(set(_step_acts(per_step))):
        pl.when(_step_has_acts(j, acts, per_step))(functools.partial(project, acts))


def _inproj(xf, mod3, nw, w_main, w_hf, lb, wga, aw, ab, *, seq, tm=1024, tn=2 * PROJ_BLOCK):
    m, d = xf.shape
    n = w_main.shape[1]
    per_batch = seq // tm
    const = lambda shape: pl.BlockSpec(shape, lambda i, j: (0, 0))
    rows = lambda r, c: pl.BlockSpec((r, c), lambda i, j: (i, 0))
    return pl.pallas_call(
        _inproj_kernel,
        grid=(m // tm, n // tn),
        in_specs=[
            rows(tm, d),
            pl.BlockSpec((1, 3, d), lambda i, j: (i // per_batch, 0, 0)),
            const((1, d)),
            pl.BlockSpec((d, tn), lambda i, j: (0, j)),
            const((d, PROJ_BLOCK)),
            const((1, d)),
            const((d, LANES)),
            const((LANES, GLA_KEY)),
            const((1, GLA_KEY)),
        ],
        out_specs=[
            pl.BlockSpec((tm, tn), lambda i, j: (i, j)),
            rows(tm, d), rows(tm, d), rows(tm // HALF, d),
            rows(tm, GLA_KEY), rows(tm // HALF, GLA_KEY),
        ],
        out_shape=[
            jax.ShapeDtypeStruct((m, n), BF16),
            jax.ShapeDtypeStruct((m, d), F32),
            jax.ShapeDtypeStruct((m, d), BF16),
            jax.ShapeDtypeStruct((m // HALF, d), F32),
            jax.ShapeDtypeStruct((m, GLA_KEY), F32),
            jax.ShapeDtypeStruct((m // HALF, GLA_KEY), F32),
        ],
        scratch_shapes=[pltpu.VMEM((tm, d), BF16)],
        compiler_params=pltpu.CompilerParams(
            dimension_semantics=("parallel", "arbitrary"),
            vmem_limit_bytes=INPROJ_VMEM_LIMIT),
        name="inproj",
    )(xf, mod3, nw, w_main, w_hf, lb, wga, aw, ab)


def _fast_masks(group):
    shape = (HALF, group * CHUNK)
    row = lax.broadcasted_iota(jnp.int32, shape, 0)
    col = lax.broadcasted_iota(jnp.int32, shape, 1) & (CHUNK - 1)
    top = col <= row
    left = col < HALF
    bot = (col >= HALF) & (col <= row + HALF)
    return top, left, bot


def _block_diag(blocks):
    n = len(blocks)
    return jnp.concatenate(
        [jnp.concatenate([b if j == i else jnp.zeros_like(b) for j in range(n)], axis=1)
         for i, b in enumerate(blocks)], axis=0)


def _block_cumsum8(g):
    g3 = g.reshape(NBLK, SUBLANES, g.shape[-1])
    sub = lax.broadcasted_iota(jnp.int32, g3.shape, 1)
    for sh in (1, 2, 4):
        g3 = g3 + jnp.where(sub >= sh, pltpu.roll(g3, sh, axis=1), 0.0)
    return g3


_NT = (((1,), (1,)), ((), ()))
_TN = (((0,), (0,)), ((), ()))


def _gla_chunk_fast(q, k, vb, g, sts, masks):
    top_mask, left_mask, bot_mask = masks
    n = len(sts)
    vdim = vb.shape[1] // n
    kcol = lambda h: slice(h * HEAD_K, (h + 1) * HEAD_K)
    vcol = lambda h: slice(h * vdim, (h + 1) * vdim)
    c8a = _block_cumsum8(g)
    c8 = [c8a[i] for i in range(NBLK)]
    t8 = [ci[SUBLANES - 1:SUBLANES, :] for ci in c8]
    per = HALF // SUBLANES
    c32, t32 = [], []
    for half in range(2):
        run = None
        for i in range(half * per, (half + 1) * per):
            c32.append(c8[i] if run is None else c8[i] + run)
            run = t8[i] if run is None else run + t8[i]
        t32.append(run)
    c32 = jnp.concatenate(c32, axis=0)
    e0 = jnp.exp(t32[0])
    e1 = jnp.exp(t32[1])
    e01 = e0 * e1
    q32 = q * jnp.exp(c32)
    kd = k * jnp.exp(-c32)
    q32b = q32.astype(BF16)
    q64_hi = (q32[HALF:] * e0).astype(BF16)
    lhs = jnp.concatenate([q32b, q64_hi], axis=0)
    kdb = kd.astype(BF16)
    sc = lax.dot_general(lhs, _block_diag([kdb[:, kcol(h)] for h in range(n)]), _NT,
                         preferred_element_type=F32)
    top = jnp.where(top_mask, sc[:HALF], 0.0)
    bot = jnp.where(left_mask, sc[CHUNK:], jnp.where(bot_mask, sc[HALF:CHUNK], 0.0))
    scores = jnp.concatenate([top, bot], axis=0).astype(BF16)

    q64 = jnp.concatenate([q32b[:HALF], q64_hi], axis=0)
    o = jnp.dot(scores, _block_diag([vb[:, vcol(h)] for h in range(n)]),
                preferred_element_type=F32)
    o = o + lax.dot_general(q64, _block_diag([st.astype(BF16) for st in sts]), _NT,
                            preferred_element_type=F32)
    k64 = jnp.concatenate([kd[:HALF] * e01, kd[HALF:] * e1], axis=0).astype(BF16)
    new_sts = [sts[h] * e01[:, kcol(h)]
               + lax.dot_general(vb[:, vcol(h)], k64[:, kcol(h)], _TN, preferred_element_type=F32)
               for h in range(n)]
    return o, new_sts


def _exact_masks():
    row = lax.broadcasted_iota(jnp.int32, (CHUNK, CHUNK), 0)
    col = lax.broadcasted_iota(jnp.int32, (CHUNK, CHUNK), 1)
    row_in = row & (SUBLANES - 1)
    row_base = row - row_in
    diag = [(col == row_base + s) & (row_in >= s) for s in range(SUBLANES)]
    levels = []
    for m in OFFDIAG_LEVELS:
        sh = m.bit_length() - 1
        tb = row >> sh
        sb = col >> sh
        levels.append(((tb & 1) == 1) & (sb == tb - 1))
    return diag, levels


def _gla_chunk_exact(q, k, vb, g, st, masks):
    diag_masks, level_masks = masks
    kdim = q.shape[-1]
    c8a = _block_cumsum8(g)

    c = {8: [c8a[i] for i in range(NBLK)]}
    t = {8: [ci[SUBLANES - 1:SUBLANES, :] for ci in c[8]]}
    m = 8
    while m < CHUNK:
        per = m // SUBLANES
        nxt = []
        for i in range(NBLK):
            mb = i // per
            nxt.append(c[m][i] + t[m][mb - 1] if mb % 2 == 1 else c[m][i])
        c[2 * m] = nxt
        t[2 * m] = [t[m][2 * j] + t[m][2 * j + 1] for j in range(len(t[m]) // 2)]
        m *= 2

    def cat(blocks):
        return jnp.concatenate(blocks, axis=0)

    def q_scaled(m):
        return (q * jnp.exp(cat(c[m]))).astype(BF16)

    def k_scaled(m):
        per = m // SUBLANES
        rest = [t[m][i // per] - c[m][i] for i in range(NBLK)]
        return (k * jnp.exp(cat(rest))).astype(BF16)

    q3 = q.reshape(NBLK, SUBLANES, kdim)
    k3 = k.reshape(NBLK, SUBLANES, kdim)
    scores = jnp.zeros((CHUNK, CHUNK), F32)
    for s in range(SUBLANES):
        e = jnp.exp(jnp.minimum(c8a - c8a[:, s:s + 1, :], 0.0))
        p = q3 * e * k3[:, s:s + 1, :]
        r = jnp.sum(p, axis=-1, keepdims=True).reshape(CHUNK, 1)
        scores = scores + jnp.where(diag_masks[s], r, 0.0)
    for m, mask in zip(OFFDIAG_LEVELS, level_masks):
        sc = lax.dot_general(q_scaled(m), k_scaled(m), _NT, preferred_element_type=F32)
        scores = scores + jnp.where(mask, sc, 0.0)

    o = jnp.dot(scores.astype(BF16), vb, preferred_element_type=F32)
    o = o + lax.dot_general(q_scaled(CHUNK), st.astype(BF16), _NT, preferred_element_type=F32)
    upd = lax.dot_general(vb, k_scaled(CHUNK), _TN, preferred_element_type=F32)
    st_new = st * jnp.exp(t[CHUNK][0]) + upd
    return o, st_new


def _scan_step(load_qkg, v_ref, gsum_ref, o_ref, st_ref, *, tb, heads, head_v, group):
    @pl.when(pl.program_id(1) == 0)
    def _():
        st_ref[...] = jnp.zeros_like(st_ref)

    safe = jnp.min(gsum_ref[...]) >= -SAFE_DECAY

    def head_chunk(chunk_fn, rows, h, masks):
        kc = slice(h * HEAD_K, (h + 1) * HEAD_K)
        vc = slice(h * head_v, (h + 1) * head_v)
        q, k, g = load_qkg(rows, kc)
        o, st_new = chunk_fn(q, k, v_ref[rows, vc], g, st_ref[h], masks)
        st_ref[h] = st_new
        o_ref[rows, vc] = o.astype(o_ref.dtype)

    def head_group_chunk(rows, h0, masks):
        hs = range(h0, h0 + group)
        kc = slice(h0 * HEAD_K, (h0 + group) * HEAD_K)
        vc = slice(h0 * head_v, (h0 + group) * head_v)
        q, k, g = load_qkg(rows, kc)
        o, new_sts = _gla_chunk_fast(q, k, v_ref[rows, vc], g, [st_ref[h] for h in hs], masks)
        for h, st_new in zip(hs, new_sts):
            st_ref[h] = st_new
        o_ref[rows, vc] = o.astype(o_ref.dtype)

    @pl.when(safe)
    def _():
        masks = _fast_masks(group)
        for c in range(tb // CHUNK):
            for h0 in range(0, heads, group):
                head_group_chunk(pl.ds(c * CHUNK, CHUNK), h0, masks)

    @pl.when(jnp.logical_not(safe))
    def _():
        masks = _exact_masks()

        def body(ci, carry):
            rows = pl.ds(pl.multiple_of(ci * CHUNK, CHUNK), CHUNK)
            for h in range(heads):
                head_chunk(_gla_chunk_exact, rows, h, masks)
            return carry

        lax.fori_loop(0, tb // CHUNK, body, 0)


def _hgrn_scan_kernel(q_ref, k_ref, v_ref, g_ref, gsum_ref, o_ref, st_ref, *, tb):
    def load_qkg(rows, kc):
        return q_ref[rows, kc].astype(F32), k_ref[rows, kc].astype(F32), g_ref[rows, kc]

    _scan_step(load_qkg, v_ref, gsum_ref, o_ref, st_ref, tb=tb, heads=HGRN_HEADS,
               head_v=HGRN_HEAD_V, group=HGRN_HEAD_GROUP)


def _gla_scan_kernel(qk_ref, v_ref, g_ref, gsum_ref, o_ref, st_ref, *, tb):
    scale = HEAD_K ** -0.5

    def load_qkg(rows, kc):
        kc2 = slice(GLA_KEY + kc.start, GLA_KEY + kc.stop)
        return (qk_ref[rows, kc].astype(F32) * scale, qk_ref[rows, kc2].astype(F32),
                g_ref[rows, kc])

    _scan_step(load_qkg, v_ref, gsum_ref, o_ref, st_ref, tb=tb, heads=GLA_HEADS,
               head_v=GLA_HEAD_V, group=GLA_HEAD_GROUP)


def _scan_call(kernel_fn, operands, in_specs, *, name, m, tb, batch, nt, heads, head_v):
    d = D_MODEL
    return pl.pallas_call(
        functools.partial(kernel_fn, tb=tb),
        grid=(batch, nt),
        in_specs=in_specs,
        out_specs=pl.BlockSpec((tb, d), lambda b, t: (b * nt + t, 0)),
        out_shape=jax.ShapeDtypeStruct((m, d), BF16),
        scratch_shapes=[pltpu.VMEM((heads, head_v, HEAD_K), F32)],
        compiler_params=pltpu.CompilerParams(dimension_semantics=("parallel", "arbitrary"),
                                             vmem_limit_bytes=SCAN_VMEM_LIMIT),
        name=name,
    )(*operands)


def _hgrn_scan(proj, k, g, gsum, *, batch, seq, tb=1024):
    nt = seq // tb
    blk = lambda r, c, col: pl.BlockSpec((r, c), lambda b, t: (b * nt + t, col))
    in_specs = [blk(tb, PROJ_BLOCK, BLK_HQ), blk(tb, PROJ_BLOCK, 0), blk(tb, PROJ_BLOCK, BLK_HI),
                blk(tb, PROJ_BLOCK, 0), blk(tb // HALF, PROJ_BLOCK, 0)]
    return _scan_call(_hgrn_scan_kernel, (proj, k, proj, g, gsum), in_specs, name="hgrn_scan",
                      m=proj.shape[0], tb=tb, batch=batch, nt=nt, heads=HGRN_HEADS,
                      head_v=HGRN_HEAD_V)


def _gla_scan(proj, loga, lsum, *, batch, seq, tb=1024):
    nt = seq // tb
    blk = lambda r, c, col: pl.BlockSpec((r, c), lambda b, t: (b * nt + t, col))
    in_specs = [blk(tb, PROJ_BLOCK, BLK_GQK), blk(tb, PROJ_BLOCK, BLK_GV),
                blk(tb, GLA_KEY, 0), blk(tb // HALF, GLA_KEY, 0)]
    return _scan_call(_gla_scan_kernel, (proj, proj, loga, lsum), in_specs, name="gla_scan",
                      m=proj.shape[0], tb=tb, batch=batch, nt=nt, heads=GLA_HEADS,
                      head_v=GLA_HEAD_V)


def _rms(x, w):
    return x * lax.rsqrt(jnp.mean(x * x, axis=-1, keepdims=True) + EPS) * w


def _out_kernel(oh_ref, og_ref, hz_ref, gz_ref, mg0_ref, mg1_ref, x_ref, mod_ref,
                hnw_ref, gnw_ref, wb0_ref, wb1_ref, wo_ref, fnw_ref, out_ref, *, final):
    f32 = lambda ref: ref[...].astype(F32)
    yh = _rms(f32(oh_ref), hnw_ref[...]) * f32(hz_ref)
    gnw = gnw_ref[...]
    og = jnp.concatenate(
        [_rms(og_ref[:, h * GLA_HEAD_V:(h + 1) * GLA_HEAD_V].astype(F32), gnw)
         for h in range(GLA_HEADS)], axis=-1)
    yg = og * f32(gz_ref)
    uh = jnp.dot(yh.astype(BF16), wb0_ref[...], preferred_element_type=F32)
    ug = jnp.dot(yg.astype(BF16), wb1_ref[...], preferred_element_type=F32)
    merged = f32(mg0_ref) * uh + f32(mg1_ref) * ug
    gate = mod_ref[0, 2:3, :]
    out = x_ref[...] + gate * jnp.dot(merged.astype(BF16), wo_ref[...],
                                      preferred_element_type=F32)
    if final:
        out = _rms(out, fnw_ref[...])
    out_ref[...] = out


def _outproj(oh, og, proj, xf, mod3, hnw, gnw, wb0, wb1, wo, fnw, *, seq, final, tm=512):
    m, d = xf.shape
    per_batch = seq // tm
    rows = lambda c: pl.BlockSpec((tm, PROJ_BLOCK), lambda i: (i, c))
    const = lambda shape: pl.BlockSpec(shape, lambda i: tuple(0 for _ in shape))
    return pl.pallas_call(
        functools.partial(_out_kernel, final=final),
        grid=(m // tm,),
        in_specs=[
            rows(0), rows(0), rows(BLK_HZ), rows(BLK_GZ), rows(BLK_MG0), rows(BLK_MG1), rows(0),
            pl.BlockSpec((1, 3, d), lambda i: (i // per_batch, 0, 0)),
            const((1, d)), const((1, GLA_HEAD_V)),
            const((d, d)), const((d, d)), const((d, d)), const((1, d)),
        ],
        out_specs=rows(0),
        out_shape=jax.ShapeDtypeStruct((m, d), F32),
        compiler_params=pltpu.CompilerParams(
            dimension_semantics=("parallel",), vmem_limit_bytes=SCAN_VMEM_LIMIT),
        name="outproj",
    )(oh, og, proj, proj, proj, proj, xf, mod3, hnw, gnw, wb0, wb1, wo, fnw)


def kernel(x, c, ada_w, ada_b, norm_w, w_in, hgrn_lb_logits, hgrn_norm_w, gla_alpha_w,
           gla_alpha_b, gla_norm_w, w_branch, w_out, final_norm_w):
    batch, seq, d = x.shape
    depth = ada_w.shape[0]
    mod_all, lb_all = _prep(c, ada_w, ada_b, hgrn_lb_logits)
    xf = x.reshape(batch * seq, d)
    ga_hi = COL_GA + GLA_GATE_RANK
    for l in range(depth):
        w_main = jnp.concatenate([w_in[l][:, :COL_HF], w_in[l][:, COL_HI:COL_GA], w_in[l][:, ga_hi:]],
                                 axis=1).astype(BF16)
        w_hf = w_in[l][:, COL_HF:COL_HI].astype(BF16)
        wga = jnp.pad(w_in[l][:, COL_GA:ga_hi], ((0, 0), (0, LANES - GLA_GATE_RANK))).astype(BF16)
        aw = jnp.pad(gla_alpha_w[l], ((0, LANES - GLA_GATE_RANK), (0, 0))).astype(BF16)
        mod3 = mod_all[l].reshape(batch, 3, d)
        proj, g, k, gsum, loga, lsum = _inproj(xf, mod3, norm_w[l][None], w_main, w_hf,
                                               lb_all[l][None], wga, aw, gla_alpha_b[l][None],
                                               seq=seq)
        oh = _hgrn_scan(proj, k, g, gsum, batch=batch, seq=seq)
        og = _gla_scan(proj, loga, lsum, batch=batch, seq=seq)
        xf = _outproj(oh, og, proj, xf, mod3, hgrn_norm_w[l][None], gla_norm_w[l][None],
                      w_branch[l, 0].astype(BF16), w_branch[l, 1].astype(BF16),
                      w_out[l].astype(BF16), final_norm_w[None],
                      seq=seq, final=(l == depth - 1))
    return xf.reshape(batch, seq, d)
```

```python
import functools

import jax
import jax.numpy as jnp
from jax import lax
from jax.experimental import pallas as pl
from jax.experimental.pallas import tpu as pltpu

F32 = jnp.float32
BF16 = jnp.bfloat16

D_MODEL = 1024
DEPTH = 2
HGRN_HEADS = 8
GLA_HEADS = 4
HEAD_K = 128
HGRN_HEAD_V = 128
GLA_HEAD_V = 256
GLA_KEY = GLA_HEADS * HEAD_K
GLA_GATE_RANK = 16
GLA_GATE_TAU = 16.0
EPS = 1e-6
F_FLOOR = 1e-20

LANES = 128
SUBLANES = 8
CHUNK = 64
NBLK = CHUNK // SUBLANES
HALF = CHUNK // 2
HGRN_HEAD_GROUP = 2
GLA_HEAD_GROUP = 1
SAFE_DECAY = 70.0
OFFDIAG_LEVELS = (8, 16, 32)

COL_HF = 1024
COL_HI = 2048
COL_GA = 7168
PROJ_BLOCK = 1024
BLK_HQ, BLK_HI, BLK_HZ, BLK_GQK, BLK_GV, BLK_GZ, BLK_MG0, BLK_MG1 = range(8)
ACT_ID, ACT_SILU, ACT_SIGMOID = range(3)
BLOCK_ACT = (ACT_SILU, ACT_ID, ACT_SILU, ACT_ID, ACT_ID, ACT_SILU, ACT_SIGMOID, ACT_SIGMOID)
SCAN_VMEM_LIMIT = 48 * 1024 * 1024
INPROJ_VMEM_LIMIT = 56 * 1024 * 1024


def _silu(x):
    return x * jax.nn.sigmoid(x)


def _mod_kernel(c_ref, w_ref, b_ref, mod_ref):
    c = c_ref[...]
    c_act = _silu(c)
    mod_ref[0] = jnp.dot(c_act, w_ref[0], preferred_element_type=F32) + b_ref[0]


def _lb_kernel(logit_ref, lb_ref):
    x = logit_ref[...]
    m = jnp.max(x, axis=0, keepdims=True)
    e = jnp.exp(x - m)
    p = e / jnp.sum(e, axis=0, keepdims=True)
    run = jnp.zeros_like(p[0:1])
    for l in range(x.shape[0]):
        run = run + p[l:l + 1]
        lb_ref[l:l + 1, :] = run - p[0:1]


def _prep(c, ada_w, ada_b, lb_logits):
    depth, d, d3 = ada_w.shape
    b = c.shape[0]
    nblk = d3 // d
    mod = pl.pallas_call(
        _mod_kernel,
        grid=(depth, nblk),
        in_specs=[
            pl.BlockSpec((b, d), lambda l, n: (0, 0)),
            pl.BlockSpec((1, d, d), lambda l, n: (l, 0, n)),
            pl.BlockSpec((1, b, d), lambda l, n: (l, 0, n)),
        ],
        out_specs=pl.BlockSpec((1, b, d), lambda l, n: (l, 0, n)),
        out_shape=jax.ShapeDtypeStruct((depth, b, d3), F32),
        name="adaln_mod",
    )(c, ada_w, jnp.broadcast_to(ada_b[:, None, :], (depth, b, d3)))
    lb = pl.pallas_call(
        _lb_kernel,
        out_shape=jax.ShapeDtypeStruct(lb_logits.shape, F32),
        name="hgrn_lower_bounds",
    )(lb_logits)
    return mod, lb


def _sigmoid(x):
    return 0.5 * jnp.tanh(0.5 * x) + 0.5


def _half_block_sums(g):
    rows, cols = g.shape
    return jnp.sum(g.reshape(rows // HALF, HALF, cols), axis=1)


def _step_acts(per_step):
    return [BLOCK_ACT[s:s + per_step] for s in range(0, len(BLOCK_ACT), per_step)]


def _step_has_acts(j, acts, per_step):
    hits = [j == s for s, a in enumerate(_step_acts(per_step)) if a == acts]
    return functools.reduce(jnp.logical_or, hits)


def _inproj_kernel(x_ref, mod_ref, nw_ref, w_ref, whf_ref, lb_ref, wga_ref, aw_ref, ab_ref,
                   out_ref, g_ref, k_ref, gsum_ref, loga_ref, lsum_ref, h_ref):
    j = pl.program_id(1)

    @pl.when(j == 0)
    def _():
        x = x_ref[...]
        ms = jnp.mean(x * x, axis=-1, keepdims=True)
        y = x * lax.rsqrt(ms + EPS) * nw_ref[...]
        shift = mod_ref[0, 0:1, :]
        scale = mod_ref[0, 1:2, :]
        hb = (y * (1.0 + scale) + shift).astype(BF16)
        h_ref[...] = hb
        lb = jnp.clip(lb_ref[...], 0.0, 1.0)
        hf = jnp.dot(hb, whf_ref[...], preferred_element_type=F32)
        f = lb + (1.0 - lb) * _sigmoid(hf)
        g = jnp.log(jnp.maximum(f, F_FLOOR))
        g_ref[...] = g
        k_ref[...] = (1.0 - f).astype(BF16)
        gsum_ref[...] = _half_block_sums(g)
        ga = jnp.dot(hb, wga_ref[...], preferred_element_type=F32)
        z = jnp.dot(ga.astype(BF16), aw_ref[...], preferred_element_type=F32) + ab_ref[...]
        log_sig = jnp.minimum(z, 0.0) - jnp.log1p(jnp.exp(-jnp.abs(z)))
        loga = log_sig * (1.0 / GLA_GATE_TAU)
        loga_ref[...] = loga
        lsum_ref[...] = _half_block_sums(loga)

    per_step = out_ref.shape[1] // PROJ_BLOCK

    def project(acts):
        for s, act in enumerate(acts):
            cols = slice(s * PROJ_BLOCK, (s + 1) * PROJ_BLOCK)
            acc = jnp.dot(h_ref[...], w_ref[:, cols], preferred_element_type=F32)
            if act != ACT_ID:
                sig = _sigmoid(acc)
                acc = acc * sig if act == ACT_SILU else sig
            out_ref[:, cols] = acc.astype(BF16)

    for acts in sorted(set(_step_acts(per_step))):
        pl.when(_step_has_acts(j, acts, per_step))(functools.partial(project, acts))


def _inproj(xf, mod3, nw, w_main, w_hf, lb, wga, aw, ab, *, seq, tm=1024, tn=2 * PROJ_BLOCK):
    m, d = xf.shape
    n = w_main.shape[1]
    per_batch = seq // tm
    const = lambda shape: pl.BlockSpec(shape, lambda i, j: (0, 0))
    rows = lambda r, c: pl.BlockSpec((r, c), lambda i, j: (i, 0))
    return pl.pallas_call(
        _inproj_kernel,
        grid=(m // tm, n // tn),
        in_specs=[
            rows(tm, d),
            pl.BlockSpec((1, 3, d), lambda i, j: (i // per_batch, 0, 0)),
            const((1, d)),
            pl.BlockSpec((d, tn), lambda i, j: (0, j)),
            const((d, PROJ_BLOCK)),
            const((1, d)),
            const((d, LANES)),
            const((LANES, GLA_KEY)),
            const((1, GLA_KEY)),
        ],
        out_specs=[
            pl.BlockSpec((tm, tn), lambda i, j: (i, j)),
            rows(tm, d), rows(tm, d), rows(tm // HALF, d),
            rows(tm, GLA_KEY), rows(tm // HALF, GLA_KEY),
        ],
        out_shape=[
            jax.ShapeDtypeStruct((m, n), BF16),
            jax.ShapeDtypeStruct((m, d), F32),
            jax.ShapeDtypeStruct((m, d), BF16),
            jax.ShapeDtypeStruct((m // HALF, d), F32),
            jax.ShapeDtypeStruct((m, GLA_KEY), F32),
            jax.ShapeDtypeStruct((m // HALF, GLA_KEY), F32),
        ],
        scratch_shapes=[pltpu.VMEM((tm, d), BF16)],
        compiler_params=pltpu.CompilerParams(
            dimension_semantics=("parallel", "arbitrary"),
            vmem_limit_bytes=INPROJ_VMEM_LIMIT),
        name="inproj",
    )(xf, mod3, nw, w_main, w_hf, lb, wga, aw, ab)


def _fast_masks(group):
    shape = (HALF, group * CHUNK)
    row = lax.broadcasted_iota(jnp.int32, shape, 0)
    col = lax.broadcasted_iota(jnp.int32, shape, 1) & (CHUNK - 1)
    top = col <= row
    left = col < HALF
    bot = (col >= HALF) & (col <= row + HALF)
    return top, left, bot


def _block_diag(blocks):
    n = len(blocks)
    return jnp.concatenate(
        [jnp.concatenate([b if j == i else jnp.zeros_like(b) for j in range(n)], axis=1)
         for i, b in enumerate(blocks)], axis=0)


def _block_cumsum8(g):
    g3 = g.reshape(NBLK, SUBLANES, g.shape[-1])
    sub = lax.broadcasted_iota(jnp.int32, g3.shape, 1)
    for sh in (1, 2, 4):
        g3 = g3 + jnp.where(sub >= sh, pltpu.roll(g3, sh, axis=1), 0.0)
    return g3


_NT = (((1,), (1,)), ((), ()))
_TN = (((0,), (0,)), ((), ()))


def _gla_chunk_fast(q, k, vb, g, st, masks, *, group):
    top_mask, left_mask, bot_mask = masks
    heads = st.shape[0]
    vdim = vb.shape[1] // heads
    ngroups = heads // group
    hcol = lambda h, w: slice(h * w, (h + 1) * w)
    gcol = lambda p, w: slice(p * group * w, (p + 1) * group * w)
    members = lambda p: range(p * group, (p + 1) * group)
    c8a = _block_cumsum8(g)
    c8 = [c8a[i] for i in range(NBLK)]
    t8 = [ci[SUBLANES - 1:SUBLANES, :] for ci in c8]
    per = HALF // SUBLANES
    c32, t32 = [], []
    for half in range(2):
        run = None
        for i in range(half * per, (half + 1) * per):
            c32.append(c8[i] if run is None else c8[i] + run)
            run = t8[i] if run is None else run + t8[i]
        t32.append(run)
    c32 = jnp.concatenate(c32, axis=0)
    e0 = jnp.exp(t32[0])
    e1 = jnp.exp(t32[1])
    e01 = e0 * e1
    q32 = q * jnp.exp(c32)
    kd = k * jnp.exp(-c32)
    q32b = q32.astype(BF16)
    q64_hi = (q32[HALF:] * e0).astype(BF16)
    lhs = jnp.concatenate([q32b, q64_hi], axis=0)
    kdb = kd.astype(BF16)
    q64 = jnp.concatenate([q32b[:HALF], q64_hi], axis=0)
    k64 = jnp.concatenate([kd[:HALF] * e01, kd[HALF:] * e1], axis=0).astype(BF16)

    batch_nt = (((2,), (2,)), ((0,), (0,)))
    batch_nn = (((2,), (1,)), ((0,), (0,)))
    batch_tn = (((1,), (1,)), ((0,), (0,)))
    lhs_b = jnp.stack([lhs[:, gcol(p, HEAD_K)] for p in range(ngroups)])
    kd_b = jnp.stack([_block_diag([kdb[:, hcol(h, HEAD_K)] for h in members(p)])
                      for p in range(ngroups)])
    sc = lax.dot_general(lhs_b, kd_b, batch_nt, preferred_element_type=F32)
    top = jnp.where(top_mask, sc[:, :HALF], 0.0)
    bot = jnp.where(left_mask, sc[:, CHUNK:], jnp.where(bot_mask, sc[:, HALF:CHUNK], 0.0))
    scores = jnp.concatenate([top, bot], axis=1).astype(BF16)
    vb_b = jnp.stack([_block_diag([vb[:, hcol(h, vdim)] for h in members(p)])
                      for p in range(ngroups)])
    st16 = st.astype(BF16)
    st_b = jnp.stack([_block_diag([st16[h] for h in members(p)]) for p in range(ngroups)])
    q64_b = jnp.stack([q64[:, gcol(p, HEAD_K)] for p in range(ngroups)])
    o = (lax.dot_general(scores, vb_b, batch_nn, preferred_element_type=F32)
         + lax.dot_general(q64_b, st_b, batch_nt, preferred_element_type=F32))
    o2d = jnp.concatenate([o[p] for p in range(ngroups)], axis=1)
    vb_h = jnp.stack([vb[:, hcol(h, vdim)] for h in range(heads)])
    k64_h = jnp.stack([k64[:, hcol(h, HEAD_K)] for h in range(heads)])
    upd = lax.dot_general(vb_h, k64_h, batch_tn, preferred_element_type=F32)
    e01_h = jnp.stack([e01[:, hcol(h, HEAD_K)] for h in range(heads)])
    return o2d, st * e01_h + upd


def _exact_masks():
    row = lax.broadcasted_iota(jnp.int32, (CHUNK, CHUNK), 0)
    col = lax.broadcasted_iota(jnp.int32, (CHUNK, CHUNK), 1)
    row_in = row & (SUBLANES - 1)
    row_base = row - row_in
    diag = [(col == row_base + s) & (row_in >= s) for s in range(SUBLANES)]
    levels = []
    for m in OFFDIAG_LEVELS:
        sh = m.bit_length() - 1
        tb = row >> sh
        sb = col >> sh
        levels.append(((tb & 1) == 1) & (sb == tb - 1))
    return diag, levels


def _gla_chunk_exact(q, k, vb, g, st, masks):
    diag_masks, level_masks = masks
    kdim = q.shape[-1]
    c8a = _block_cumsum8(g)

    c = {8: [c8a[i] for i in range(NBLK)]}
    t = {8: [ci[SUBLANES - 1:SUBLANES, :] for ci in c[8]]}
    m = 8
    while m < CHUNK:
        per = m // SUBLANES
        nxt = []
        for i in range(NBLK):
            mb = i // per
            nxt.append(c[m][i] + t[m][mb - 1] if mb % 2 == 1 else c[m][i])
        c[2 * m] = nxt
        t[2 * m] = [t[m][2 * j] + t[m][2 * j + 1] for j in range(len(t[m]) // 2)]
        m *= 2

    def cat(blocks):
        return jnp.concatenate(blocks, axis=0)

    def q_scaled(m):
        return (q * jnp.exp(cat(c[m]))).astype(BF16)

    def k_scaled(m):
        per = m // SUBLANES
        rest = [t[m][i // per] - c[m][i] for i in range(NBLK)]
        return (k * jnp.exp(cat(rest))).astype(BF16)

    q3 = q.reshape(NBLK, SUBLANES, kdim)
    k3 = k.reshape(NBLK, SUBLANES, kdim)
    scores = jnp.zeros((CHUNK, CHUNK), F32)
    for s in range(SUBLANES):
        e = jnp.exp(jnp.minimum(c8a - c8a[:, s:s + 1, :], 0.0))
        p = q3 * e * k3[:, s:s + 1, :]
        r = jnp.sum(p, axis=-1, keepdims=True).reshape(CHUNK, 1)
        scores = scores + jnp.where(diag_masks[s], r, 0.0)
    for m, mask in zip(OFFDIAG_LEVELS, level_masks):
        sc = lax.dot_general(q_scaled(m), k_scaled(m), _NT, preferred_element_type=F32)
        scores = scores + jnp.where(mask, sc, 0.0)

    o = jnp.dot(scores.astype(BF16), vb, preferred_element_type=F32)
    o = o + lax.dot_general(q_scaled(CHUNK), st.astype(BF16), _NT, preferred_element_type=F32)
    upd = lax.dot_general(vb, k_scaled(CHUNK), _TN, preferred_element_type=F32)
    st_new = st * jnp.exp(t[CHUNK][0]) + upd
    return o, st_new


def _scan_step(load_qkg, v_ref, gsum_ref, o_ref, st_ref, *, tb, heads, head_v, group):
    @pl.when(pl.program_id(1) == 0)
    def _():
        st_ref[...] = jnp.zeros_like(st_ref)

    safe = jnp.min(gsum_ref[...]) >= -SAFE_DECAY

    def head_chunk(chunk_fn, rows, h, masks):
        kc = slice(h * HEAD_K, (h + 1) * HEAD_K)
        vc = slice(h * head_v, (h + 1) * head_v)
        q, k, g = load_qkg(rows, kc)
        o, st_new = chunk_fn(q, k, v_ref[rows, vc], g, st_ref[h], masks)
        st_ref[h] = st_new
        o_ref[rows, vc] = o.astype(o_ref.dtype)

    @pl.when(safe)
    def _():
        masks = _fast_masks(group)
        for c in range(tb // CHUNK):
            rows = pl.ds(c * CHUNK, CHUNK)
            q, k, g = load_qkg(rows, slice(0, heads * HEAD_K))
            o, st_new = _gla_chunk_fast(q, k, v_ref[rows, :], g, st_ref[...], masks, group=group)
            st_ref[...] = st_new
            o_ref[rows, :] = o.astype(o_ref.dtype)

    @pl.when(jnp.logical_not(safe))
    def _():
        masks = _exact_masks()

        def body(ci, carry):
            rows = pl.ds(pl.multiple_of(ci * CHUNK, CHUNK), CHUNK)
            for h in range(heads):
                head_chunk(_gla_chunk_exact, rows, h, masks)
            return carry

        lax.fori_loop(0, tb // CHUNK, body, 0)


def _hgrn_scan_kernel(q_ref, k_ref, v_ref, g_ref, gsum_ref, o_ref, st_ref, *, tb):
    def load_qkg(rows, kc):
        return q_ref[rows, kc].astype(F32), k_ref[rows, kc].astype(F32), g_ref[rows, kc]

    _scan_step(load_qkg, v_ref, gsum_ref, o_ref, st_ref, tb=tb, heads=HGRN_HEADS,
               head_v=HGRN_HEAD_V, group=HGRN_HEAD_GROUP)


def _gla_scan_kernel(qk_ref, v_ref, g_ref, gsum_ref, o_ref, st_ref, *, tb):
    scale = HEAD_K ** -0.5

    def load_qkg(rows, kc):
        kc2 = slice(GLA_KEY + kc.start, GLA_KEY + kc.stop)
        return (qk_ref[rows, kc].astype(F32) * scale, qk_ref[rows, kc2].astype(F32),
                g_ref[rows, kc])

    _scan_step(load_qkg, v_ref, gsum_ref, o_ref, st_ref, tb=tb, heads=GLA_HEADS,
               head_v=GLA_HEAD_V, group=GLA_HEAD_GROUP)


def _scan_call(kernel_fn, operands, in_specs, *, name, m, tb, batch, nt, heads, head_v):
    d = D_MODEL
    return pl.pallas_call(
        functools.partial(kernel_fn, tb=tb),
        grid=(batch, nt),
        in_specs=in_specs,
        out_specs=pl.BlockSpec((tb, d), lambda b, t: (b * nt + t, 0)),
        out_shape=jax.ShapeDtypeStruct((m, d), BF16),
        scratch_shapes=[pltpu.VMEM((heads, head_v, HEAD_K), F32)],
        compiler_params=pltpu.CompilerParams(dimension_semantics=("parallel", "arbitrary"),
                                             vmem_limit_bytes=SCAN_VMEM_LIMIT),
        name=name,
    )(*operands)


def _hgrn_scan(proj, k, g, gsum, *, batch, seq, tb=1024):
    nt = seq // tb
    blk = lambda r, c, col: pl.BlockSpec((r, c), lambda b, t: (b * nt + t, col))
    in_specs = [blk(tb, PROJ_BLOCK, BLK_HQ), blk(tb, PROJ_BLOCK, 0), blk(tb, PROJ_BLOCK, BLK_HI),
                blk(tb, PROJ_BLOCK, 0), blk(tb // HALF, PROJ_BLOCK, 0)]
    return _scan_call(_hgrn_scan_kernel, (proj, k, proj, g, gsum), in_specs, name="hgrn_scan",
                      m=proj.shape[0], tb=tb, batch=batch, nt=nt, heads=HGRN_HEADS,
                      head_v=HGRN_HEAD_V)


def _gla_scan(proj, loga, lsum, *, batch, seq, tb=1024):
    nt = seq // tb
    blk = lambda r, c, col: pl.BlockSpec((r, c), lambda b, t: (b * nt + t, col))
    in_specs = [blk(tb, PROJ_BLOCK, BLK_GQK), blk(tb, PROJ_BLOCK, BLK_GV),
                blk(tb, GLA_KEY, 0), blk(tb // HALF, GLA_KEY, 0)]
    return _scan_call(_gla_scan_kernel, (proj, proj, loga, lsum), in_specs, name="gla_scan",
                      m=proj.shape[0], tb=tb, batch=batch, nt=nt, heads=GLA_HEADS,
                      head_v=GLA_HEAD_V)


def _rms(x, w):
    return x * lax.rsqrt(jnp.mean(x * x, axis=-1, keepdims=True) + EPS) * w


def _out_kernel(oh_ref, og_ref, hz_ref, gz_ref, mg0_ref, mg1_ref, x_ref, mod_ref,
                hnw_ref, gnw_ref, wb0_ref, wb1_ref, wo_ref, fnw_ref, out_ref, *, final):
    f32 = lambda ref: ref[...].astype(F32)
    yh = _rms(f32(oh_ref), hnw_ref[...]) * f32(hz_ref)
    gnw = gnw_ref[...]
    og = jnp.concatenate(
        [_rms(og_ref[:, h * GLA_HEAD_V:(h + 1) * GLA_HEAD_V].astype(F32), gnw)
         for h in range(GLA_HEADS)], axis=-1)
    yg = og * f32(gz_ref)
    uh = jnp.dot(yh.astype(BF16), wb0_ref[...], preferred_element_type=F32)
    ug = jnp.dot(yg.astype(BF16), wb1_ref[...], preferred_element_type=F32)
    merged = f32(mg0_ref) * uh + f32(mg1_ref) * ug
    gate = mod_ref[0, 2:3, :]
    out = x_ref[...] + gate * jnp.dot(merged.astype(BF16), wo_ref[...],
                                      preferred_element_type=F32)
    if final:
        out = _rms(out, fnw_ref[...])
    out_ref[...] = out


def _outproj(oh, og, proj, xf, mod3, hnw, gnw, wb0, wb1, wo, fnw, *, seq, final, tm=512):
    m, d = xf.shape
    per_batch = seq // tm
    rows = lambda c: pl.BlockSpec((tm, PROJ_BLOCK), lambda i: (i, c))
    const = lambda shape: pl.BlockSpec(shape, lambda i: tuple(0 for _ in shape))
    return pl.pallas_call(
        functools.partial(_out_kernel, final=final),
        grid=(m // tm,),
        in_specs=[
            rows(0), rows(0), rows(BLK_HZ), rows(BLK_GZ), rows(BLK_MG0), rows(BLK_MG1), rows(0),
            pl.BlockSpec((1, 3, d), lambda i: (i // per_batch, 0, 0)),
            const((1, d)), const((1, GLA_HEAD_V)),
            const((d, d)), const((d, d)), const((d, d)), const((1, d)),
        ],
        out_specs=rows(0),
        out_shape=jax.ShapeDtypeStruct((m, d), F32),
        compiler_params=pltpu.CompilerParams(
            dimension_semantics=("parallel",), vmem_limit_bytes=SCAN_VMEM_LIMIT),
        name="outproj",
    )(oh, og, proj, proj, proj, proj, xf, mod3, hnw, gnw, wb0, wb1, wo, fnw)


def kernel(x, c, ada_w, ada_b, norm_w, w_in, hgrn_lb_logits, hgrn_norm_w, gla_alpha_w,
           gla_alpha_b, gla_norm_w, w_branch, w_out, final_norm_w):
    batch, seq, d = x.shape
    depth = ada_w.shape[0]
    mod_all, lb_all = _prep(c, ada_w, ada_b, hgrn_lb_logits)
    xf = x.reshape(batch * seq, d)
    ga_hi = COL_GA + GLA_GATE_RANK
    for l in range(depth):
        w_main = jnp.concatenate([w_in[l][:, :COL_HF], w_in[l][:, COL_HI:COL_GA], w_in[l][:, ga_hi:]],
                                 axis=1).astype(BF16)
        w_hf = w_in[l][:, COL_HF:COL_HI].astype(BF16)
        wga = jnp.pad(w_in[l][:, COL_GA:ga_hi], ((0, 0), (0, LANES - GLA_GATE_RANK))).astype(BF16)
        aw = jnp.pad(gla_alpha_w[l], ((0, LANES - GLA_GATE_RANK), (0, 0))).astype(BF16)
        mod3 = mod_all[l].reshape(batch, 3, d)
        proj, g, k, gsum, loga, lsum = _inproj(xf, mod3, norm_w[l][None], w_main, w_hf,
                                               lb_all[l][None], wga, aw, gla_alpha_b[l][None],
                                               seq=seq)
        oh = _hgrn_scan(proj, k, g, gsum, batch=batch, seq=seq)
        og = _gla_scan(proj, loga, lsum, batch=batch, seq=seq)
        xf = _outproj(oh, og, proj, xf, mod3, hgrn_norm_w[l][None], gla_norm_w[l][None],
                      w_branch[l, 0].astype(BF16), w_branch[l, 1].astype(BF16),
                      w_out[l].astype(BF16), final_norm_w[None],
                      seq=seq, final=(l == depth - 1))
    return xf.reshape(batch, seq, d)
```

```python
import functools

import jax
import jax.numpy as jnp
from jax import lax
from jax.experimental import pallas as pl
from jax.experimental.pallas import tpu as pltpu

F32 = jnp.float32
BF16 = jnp.bfloat16

D_MODEL = 1024
DEPTH = 2
HGRN_HEADS = 8
GLA_HEADS = 4
HEAD_K = 128
HGRN_HEAD_V = 128
GLA_HEAD_V = 256
GLA_KEY = GLA_HEADS * HEAD_K
GLA_GATE_RANK = 16
GLA_GATE_TAU = 16.0
EPS = 1e-6
F_FLOOR = 1e-20

LANES = 128
SUBLANES = 8
CHUNK = 64
NBLK = CHUNK // SUBLANES
HALF = CHUNK // 2
HGRN_HEAD_GROUP = 2
GLA_HEAD_GROUP = 1
SAFE_DECAY = 70.0
OFFDIAG_LEVELS = (8, 16, 32)

COL_HF = 1024
COL_HI = 2048
COL_GA = 7168
PROJ_BLOCK = 1024
BLK_HQ, BLK_HI, BLK_HZ, BLK_GQK, BLK_GV, BLK_GZ, BLK_MG0, BLK_MG1 = range(8)
ACT_ID, ACT_SILU, ACT_SIGMOID = range(3)
BLOCK_ACT = (ACT_SILU, ACT_ID, ACT_SILU, ACT_ID, ACT_ID, ACT_SILU, ACT_SIGMOID, ACT_SIGMOID)
OUT_VMEM_LIMIT = 48 * 1024 * 1024
PROJ_VMEM_LIMIT = 56 * 1024 * 1024


def _silu(x):
    return x * jax.nn.sigmoid(x)


def _mod_kernel(c_ref, w_ref, b_ref, mod_ref):
    c = c_ref[...]
    c_act = _silu(c)
    mod_ref[0] = jnp.dot(c_act.astype(BF16), w_ref[0].astype(BF16),
                         preferred_element_type=F32) + b_ref[0]


def _lb_kernel(logit_ref, lb_ref):
    x = logit_ref[...]
    m = jnp.max(x, axis=0, keepdims=True)
    e = jnp.exp(x - m)
    p = e / jnp.sum(e, axis=0, keepdims=True)
    run = jnp.zeros_like(p[0:1])
    for l in range(x.shape[0]):
        run = run + p[l:l + 1]
        lb_ref[l:l + 1, :] = run - p[0:1]


def _prep(c, ada_w, ada_b, lb_logits):
    depth, d, d3 = ada_w.shape
    b = c.shape[0]
    nblk = d3 // d
    mod = pl.pallas_call(
        _mod_kernel,
        grid=(depth, nblk),
        in_specs=[
            pl.BlockSpec((b, d), lambda l, n: (0, 0)),
            pl.BlockSpec((1, d, d), lambda l, n: (l, 0, n)),
            pl.BlockSpec((1, b, d), lambda l, n: (l, 0, n)),
        ],
        out_specs=pl.BlockSpec((1, b, d), lambda l, n: (l, 0, n)),
        out_shape=jax.ShapeDtypeStruct((depth, b, d3), F32),
        name="adaln_mod",
    )(c, ada_w, jnp.broadcast_to(ada_b[:, None, :], (depth, b, d3)))
    lb = pl.pallas_call(
        _lb_kernel,
        out_shape=jax.ShapeDtypeStruct(lb_logits.shape, F32),
        name="hgrn_lower_bounds",
    )(lb_logits)
    return mod, lb


def _sigmoid(x):
    return 0.5 * jnp.tanh(0.5 * x) + 0.5


def _half_block_sums(g):
    rows, cols = g.shape
    return jnp.sum(g.reshape(rows // HALF, HALF, cols), axis=1)


def _step_acts(per_step):
    return [BLOCK_ACT[s:s + per_step] for s in range(0, len(BLOCK_ACT), per_step)]


def _step_has_acts(j, acts, per_step):
    hits = [j == s for s, a in enumerate(_step_acts(per_step)) if a == acts]
    return functools.reduce(jnp.logical_or, hits)


def _inproj_kernel(x_ref, mod_ref, nw_ref, w_ref, whf_ref, lb_ref, wga_ref, aw_ref, ab_ref,
                   out_ref, g_ref, k_ref, gsum_ref, loga_ref, lsum_ref, h_ref):
    j = pl.program_id(1)

    @pl.when(j == 0)
    def _():
        x = x_ref[...]
        ms = jnp.mean(x * x, axis=-1, keepdims=True)
        y = x * lax.rsqrt(ms + EPS) * nw_ref[...]
        shift = mod_ref[0, 0:1, :]
        scale = mod_ref[0, 1:2, :]
        hb = (y * (1.0 + scale) + shift).astype(BF16)
        h_ref[...] = hb
        lb = jnp.clip(lb_ref[...], 0.0, 1.0)
        hf = jnp.dot(hb, whf_ref[...], preferred_element_type=F32)
        f = lb + (1.0 - lb) * _sigmoid(hf)
        g = jnp.log(jnp.maximum(f, F_FLOOR))
        g_ref[...] = g
        k_ref[...] = (1.0 - f).astype(BF16)
        gsum_ref[...] = _half_block_sums(g)
        ga = jnp.dot(hb, wga_ref[...], preferred_element_type=F32)
        z = jnp.dot(ga.astype(BF16), aw_ref[...], preferred_element_type=F32) + ab_ref[...]
        log_sig = jnp.minimum(z, 0.0) - jnp.log1p(jnp.exp(-jnp.abs(z)))
        loga = log_sig * (1.0 / GLA_GATE_TAU)
        loga_ref[...] = loga
        lsum_ref[...] = _half_block_sums(loga)

    per_step = out_ref.shape[1] // PROJ_BLOCK

    def project(acts):
        for s, act in enumerate(acts):
            cols = slice(s * PROJ_BLOCK, (s + 1) * PROJ_BLOCK)
            acc = jnp.dot(h_ref[...], w_ref[:, cols], preferred_element_type=F32)
            if act != ACT_ID:
                sig = _sigmoid(acc)
                acc = acc * sig if act == ACT_SILU else sig
            out_ref[:, cols] = acc.astype(BF16)

    for acts in sorted(set(_step_acts(per_step))):
        pl.when(_step_has_acts(j, acts, per_step))(functools.partial(project, acts))


def _inproj(xf, mod3, nw, w_main, w_hf, lb, wga, aw, ab, *, seq, tm=1024, tn=2 * PROJ_BLOCK):
    m, d = xf.shape
    n = w_main.shape[1]
    per_batch = seq // tm
    const = lambda shape: pl.BlockSpec(shape, lambda i, j: (0, 0))
    rows = lambda r, c: pl.BlockSpec((r, c), lambda i, j: (i, 0))
    return pl.pallas_call(
        _inproj_kernel,
        grid=(m // tm, n // tn),
        in_specs=[
            rows(tm, d),
            pl.BlockSpec((1, 3, d), lambda i, j: (i // per_batch, 0, 0)),
            const((1, d)),
            pl.BlockSpec((d, tn), lambda i, j: (0, j)),
            const((d, PROJ_BLOCK)),
            const((1, d)),
            const((d, LANES)),
            const((LANES, GLA_KEY)),
            const((1, GLA_KEY)),
        ],
        out_specs=[
            pl.BlockSpec((tm, tn), lambda i, j: (i, j)),
            rows(tm, d), rows(tm, d), rows(tm // HALF, d),
            rows(tm, GLA_KEY), rows(tm // HALF, GLA_KEY),
        ],
        out_shape=[
            jax.ShapeDtypeStruct((m, n), BF16),
            jax.ShapeDtypeStruct((m, d), F32),
            jax.ShapeDtypeStruct((m, d), BF16),
            jax.ShapeDtypeStruct((m // HALF, d), F32),
            jax.ShapeDtypeStruct((m, GLA_KEY), F32),
            jax.ShapeDtypeStruct((m // HALF, GLA_KEY), F32),
        ],
        scratch_shapes=[pltpu.VMEM((tm, d), BF16)],
        compiler_params=pltpu.CompilerParams(
            dimension_semantics=("parallel", "arbitrary"),
            vmem_limit_bytes=PROJ_VMEM_LIMIT),
        name="inproj",
    )(xf, mod3, nw, w_main, w_hf, lb, wga, aw, ab)


def _fast_masks(group):
    shape = (HALF, group * CHUNK)
    row = lax.broadcasted_iota(jnp.int32, shape, 0)
    col = lax.broadcasted_iota(jnp.int32, shape, 1) & (CHUNK - 1)
    top = col <= row
    left = col < HALF
    bot = (col >= HALF) & (col <= row + HALF)
    return top, left, bot


def _block_diag(blocks):
    n = len(blocks)
    return jnp.concatenate(
        [jnp.concatenate([b if j == i else jnp.zeros_like(b) for j in range(n)], axis=1)
         for i, b in enumerate(blocks)], axis=0)


def _block_cumsum8(g):
    g3 = g.reshape(NBLK, SUBLANES, g.shape[-1])
    sub = lax.broadcasted_iota(jnp.int32, g3.shape, 1)
    for sh in (1, 2, 4):
        g3 = g3 + jnp.where(sub >= sh, pltpu.roll(g3, sh, axis=1), 0.0)
    return g3


_NT = (((1,), (1,)), ((), ()))
_TN = (((0,), (0,)), ((), ()))


def _gla_chunk_fast(q, k, vb, g, st, masks, *, group):
    top_mask, left_mask, bot_mask = masks
    heads = st.shape[0]
    vdim = vb.shape[1] // heads
    ngroups = heads // group
    hcol = lambda h, w: slice(h * w, (h + 1) * w)
    gcol = lambda p, w: slice(p * group * w, (p + 1) * group * w)
    members = lambda p: range(p * group, (p + 1) * group)
    c8a = _block_cumsum8(g)
    c8 = [c8a[i] for i in range(NBLK)]
    t8 = [ci[SUBLANES - 1:SUBLANES, :] for ci in c8]
    per = HALF // SUBLANES
    c32, t32 = [], []
    for half in range(2):
        run = None
        for i in range(half * per, (half + 1) * per):
            c32.append(c8[i] if run is None else c8[i] + run)
            run = t8[i] if run is None else run + t8[i]
        t32.append(run)
    c32 = jnp.concatenate(c32, axis=0)
    e0 = jnp.exp(t32[0])
    e1 = jnp.exp(t32[1])
    e01 = e0 * e1
    q32 = q * jnp.exp(c32)
    kd = k * jnp.exp(-c32)
    q32b = q32.astype(BF16)
    q64_hi = (q32[HALF:] * e0).astype(BF16)
    lhs = jnp.concatenate([q32b, q64_hi], axis=0)
    kdb = kd.astype(BF16)
    q64 = jnp.concatenate([q32b[:HALF], q64_hi], axis=0)
    k64 = jnp.concatenate([kd[:HALF] * e01, kd[HALF:] * e1], axis=0).astype(BF16)

    batch_nt = (((2,), (2,)), ((0,), (0,)))
    batch_nn = (((2,), (1,)), ((0,), (0,)))
    batch_tn = (((1,), (1,)), ((0,), (0,)))
    lhs_b = jnp.stack([lhs[:, gcol(p, HEAD_K)] for p in range(ngroups)])
    kd_b = jnp.stack([_block_diag([kdb[:, hcol(h, HEAD_K)] for h in members(p)])
                      for p in range(ngroups)])
    sc = lax.dot_general(lhs_b, kd_b, batch_nt, preferred_element_type=F32)
    top = jnp.where(top_mask, sc[:, :HALF], 0.0)
    bot = jnp.where(left_mask, sc[:, CHUNK:], jnp.where(bot_mask, sc[:, HALF:CHUNK], 0.0))
    scores = jnp.concatenate([top, bot], axis=1).astype(BF16)
    vb_b = jnp.stack([_block_diag([vb[:, hcol(h, vdim)] for h in members(p)])
                      for p in range(ngroups)])
    st16 = st.astype(BF16)
    st_b = jnp.stack([_block_diag([st16[h] for h in members(p)]) for p in range(ngroups)])
    q64_b = jnp.stack([q64[:, gcol(p, HEAD_K)] for p in range(ngroups)])
    o = (lax.dot_general(scores, vb_b, batch_nn, preferred_element_type=F32)
         + lax.dot_general(q64_b, st_b, batch_nt, preferred_element_type=F32))
    o2d = jnp.concatenate([o[p] for p in range(ngroups)], axis=1)
    vb_h = jnp.stack([vb[:, hcol(h, vdim)] for h in range(heads)])
    k64_h = jnp.stack([k64[:, hcol(h, HEAD_K)] for h in range(heads)])
    upd = lax.dot_general(vb_h, k64_h, batch_tn, preferred_element_type=F32)
    e01_h = jnp.stack([e01[:, hcol(h, HEAD_K)] for h in range(heads)])
    return o2d, st * e01_h + upd


def _exact_masks():
    row = lax.broadcasted_iota(jnp.int32, (CHUNK, CHUNK), 0)
    col = lax.broadcasted_iota(jnp.int32, (CHUNK, CHUNK), 1)
    row_in = row & (SUBLANES - 1)
    row_base = row - row_in
    diag = [(col == row_base + s) & (row_in >= s) for s in range(SUBLANES)]
    levels = []
    for m in OFFDIAG_LEVELS:
        sh = m.bit_length() - 1
        tb = row >> sh
        sb = col >> sh
        levels.append(((tb & 1) == 1) & (sb == tb - 1))
    return diag, levels


def _gla_chunk_exact(q, k, vb, g, st, masks):
    diag_masks, level_masks = masks
    kdim = q.shape[-1]
    c8a = _block_cumsum8(g)

    c = {8: [c8a[i] for i in range(NBLK)]}
    t = {8: [ci[SUBLANES - 1:SUBLANES, :] for ci in c[8]]}
    m = 8
    while m < CHUNK:
        per = m // SUBLANES
        nxt = []
        for i in range(NBLK):
            mb = i // per
            nxt.append(c[m][i] + t[m][mb - 1] if mb % 2 == 1 else c[m][i])
        c[2 * m] = nxt
        t[2 * m] = [t[m][2 * j] + t[m][2 * j + 1] for j in range(len(t[m]) // 2)]
        m *= 2

    def cat(blocks):
        return jnp.concatenate(blocks, axis=0)

    def q_scaled(m):
        return (q * jnp.exp(cat(c[m]))).astype(BF16)

    def k_scaled(m):
        per = m // SUBLANES
        rest = [t[m][i // per] - c[m][i] for i in range(NBLK)]
        return (k * jnp.exp(cat(rest))).astype(BF16)

    q3 = q.reshape(NBLK, SUBLANES, kdim)
    k3 = k.reshape(NBLK, SUBLANES, kdim)
    scores = jnp.zeros((CHUNK, CHUNK), F32)
    for s in range(SUBLANES):
        e = jnp.exp(jnp.minimum(c8a - c8a[:, s:s + 1, :], 0.0))
        p = q3 * e * k3[:, s:s + 1, :]
        r = jnp.sum(p, axis=-1, keepdims=True).reshape(CHUNK, 1)
        scores = scores + jnp.where(diag_masks[s], r, 0.0)
    for m, mask in zip(OFFDIAG_LEVELS, level_masks):
        sc = lax.dot_general(q_scaled(m), k_scaled(m), _NT, preferred_element_type=F32)
        scores = scores + jnp.where(mask, sc, 0.0)

    o = jnp.dot(scores.astype(BF16), vb, preferred_element_type=F32)
    o = o + lax.dot_general(q_scaled(CHUNK), st.astype(BF16), _NT, preferred_element_type=F32)
    upd = lax.dot_general(vb, k_scaled(CHUNK), _TN, preferred_element_type=F32)
    st_new = st * jnp.exp(t[CHUNK][0]) + upd
    return o, st_new


def _scan_step(load_qkg, v_ref, gsum_ref, o_ref, st_ref, *, tb, heads, head_v, group):
    @pl.when(pl.program_id(1) == 0)
    def _():
        st_ref[...] = jnp.zeros_like(st_ref)

    safe = jnp.min(gsum_ref[...]) >= -SAFE_DECAY

    def head_chunk(chunk_fn, rows, h, masks):
        kc = slice(h * HEAD_K, (h + 1) * HEAD_K)
        vc = slice(h * head_v, (h + 1) * head_v)
        q, k, g = load_qkg(rows, kc)
        o, st_new = chunk_fn(q, k, v_ref[rows, vc], g, st_ref[h], masks)
        st_ref[h] = st_new
        o_ref[rows, vc] = o.astype(o_ref.dtype)

    @pl.when(safe)
    def _():
        masks = _fast_masks(group)
        for c in range(tb // CHUNK):
            rows = pl.ds(c * CHUNK, CHUNK)
            q, k, g = load_qkg(rows, slice(0, heads * HEAD_K))
            o, st_new = _gla_chunk_fast(q, k, v_ref[rows, :], g, st_ref[...], masks, group=group)
            st_ref[...] = st_new
            o_ref[rows, :] = o.astype(o_ref.dtype)

    @pl.when(jnp.logical_not(safe))
    def _():
        masks = _exact_masks()

        def body(ci, carry):
            rows = pl.ds(pl.multiple_of(ci * CHUNK, CHUNK), CHUNK)
            for h in range(heads):
                head_chunk(_gla_chunk_exact, rows, h, masks)
            return carry

        lax.fori_loop(0, tb // CHUNK, body, 0)


def _scan_kernel(hq_ref, hk_ref, hv_ref, hg_ref, hgsum_ref, gqk_ref, gv_ref, gg_ref, ggsum_ref,
                 oh_ref, og_ref, hst_ref, gst_ref, *, tb):
    def load_hgrn(rows, kc):
        return hq_ref[rows, kc].astype(F32), hk_ref[rows, kc].astype(F32), hg_ref[rows, kc]

    _scan_step(load_hgrn, hv_ref, hgsum_ref, oh_ref, hst_ref, tb=tb, heads=HGRN_HEADS,
               head_v=HGRN_HEAD_V, group=HGRN_HEAD_GROUP)

    scale = HEAD_K ** -0.5

    def load_gla(rows, kc):
        kc2 = slice(GLA_KEY + kc.start, GLA_KEY + kc.stop)
        return (gqk_ref[rows, kc].astype(F32) * scale, gqk_ref[rows, kc2].astype(F32),
                gg_ref[rows, kc])

    _scan_step(load_gla, gv_ref, ggsum_ref, og_ref, gst_ref, tb=tb, heads=GLA_HEADS,
               head_v=GLA_HEAD_V, group=GLA_HEAD_GROUP)


def _scans(proj, k, g, gsum, loga, lsum, *, batch, seq, tb=1024):
    m = proj.shape[0]
    d = D_MODEL
    nt = seq // tb
    blk = lambda r, c, col: pl.BlockSpec((r, c), lambda b, t: (b * nt + t, col))
    out = jax.ShapeDtypeStruct((m, d), BF16)
    return pl.pallas_call(
        functools.partial(_scan_kernel, tb=tb),
        grid=(batch, nt),
        in_specs=[blk(tb, PROJ_BLOCK, BLK_HQ), blk(tb, PROJ_BLOCK, 0), blk(tb, PROJ_BLOCK, BLK_HI),
                  blk(tb, PROJ_BLOCK, 0), blk(tb // HALF, PROJ_BLOCK, 0),
                  blk(tb, PROJ_BLOCK, BLK_GQK), blk(tb, PROJ_BLOCK, BLK_GV),
                  blk(tb, GLA_KEY, 0), blk(tb // HALF, GLA_KEY, 0)],
        out_specs=[blk(tb, d, 0), blk(tb, d, 0)],
        out_shape=[out, out],
        scratch_shapes=[pltpu.VMEM((HGRN_HEADS, HGRN_HEAD_V, HEAD_K), F32),
                        pltpu.VMEM((GLA_HEADS, GLA_HEAD_V, HEAD_K), F32)],
        compiler_params=pltpu.CompilerParams(dimension_semantics=("parallel", "arbitrary"),
                                             vmem_limit_bytes=PROJ_VMEM_LIMIT),
        name="scans",
    )(proj, k, proj, g, gsum, proj, proj, loga, lsum)


def _rms(x, w):
    return x * lax.rsqrt(jnp.mean(x * x, axis=-1, keepdims=True) + EPS) * w


def _out_kernel(oh_ref, og_ref, hz_ref, gz_ref, mg0_ref, mg1_ref, x_ref, mod_ref,
                hnw_ref, gnw_ref, wb0_ref, wb1_ref, wo_ref, fnw_ref, out_ref, *, final):
    f32 = lambda ref: ref[...].astype(F32)
    yh = _rms(f32(oh_ref), hnw_ref[...]) * f32(hz_ref)
    gnw = gnw_ref[...]
    og = jnp.concatenate(
        [_rms(og_ref[:, h * GLA_HEAD_V:(h + 1) * GLA_HEAD_V].astype(F32), gnw)
         for h in range(GLA_HEADS)], axis=-1)
    yg = og * f32(gz_ref)
    uh = jnp.dot(yh.astype(BF16), wb0_ref[...], preferred_element_type=F32)
    ug = jnp.dot(yg.astype(BF16), wb1_ref[...], preferred_element_type=F32)
    merged = f32(mg0_ref) * uh + f32(mg1_ref) * ug
    gate = mod_ref[0, 2:3, :]
    out = x_ref[...] + gate * jnp.dot(merged.astype(BF16), wo_ref[...],
                                      preferred_element_type=F32)
    if final:
        out = _rms(out, fnw_ref[...])
    out_ref[...] = out


def _outproj(oh, og, proj, xf, mod3, hnw, gnw, wb0, wb1, wo, fnw, *, seq, final, tm=512):
    m, d = xf.shape
    per_batch = seq // tm
    rows = lambda c: pl.BlockSpec((tm, PROJ_BLOCK), lambda i: (i, c))
    const = lambda shape: pl.BlockSpec(shape, lambda i: tuple(0 for _ in shape))
    return pl.pallas_call(
        functools.partial(_out_kernel, final=final),
        grid=(m // tm,),
        in_specs=[
            rows(0), rows(0), rows(BLK_HZ), rows(BLK_GZ), rows(BLK_MG0), rows(BLK_MG1), rows(0),
            pl.BlockSpec((1, 3, d), lambda i: (i // per_batch, 0, 0)),
            const((1, d)), const((1, GLA_HEAD_V)),
            const((d, d)), const((d, d)), const((d, d)), const((1, d)),
        ],
        out_specs=rows(0),
        out_shape=jax.ShapeDtypeStruct((m, d), F32),
        compiler_params=pltpu.CompilerParams(
            dimension_semantics=("parallel",), vmem_limit_bytes=OUT_VMEM_LIMIT),
        name="outproj",
    )(oh, og, proj, proj, proj, proj, xf, mod3, hnw, gnw, wb0, wb1, wo, fnw)


def kernel(x, c, ada_w, ada_b, norm_w, w_in, hgrn_lb_logits, hgrn_norm_w, gla_alpha_w,
           gla_alpha_b, gla_norm_w, w_branch, w_out, final_norm_w):
    batch, seq, d = x.shape
    depth = ada_w.shape[0]
    mod_all, lb_all = _prep(c, ada_w, ada_b, hgrn_lb_logits)
    xf = x.reshape(batch * seq, d)
    ga_hi = COL_GA + GLA_GATE_RANK
    for l in range(depth):
        w_main = jnp.concatenate([w_in[l][:, :COL_HF], w_in[l][:, COL_HI:COL_GA], w_in[l][:, ga_hi:]],
                                 axis=1).astype(BF16)
        w_hf = w_in[l][:, COL_HF:COL_HI].astype(BF16)
        wga = jnp.pad(w_in[l][:, COL_GA:ga_hi], ((0, 0), (0, LANES - GLA_GATE_RANK))).astype(BF16)
        aw = jnp.pad(gla_alpha_w[l], ((0, LANES - GLA_GATE_RANK), (0, 0))).astype(BF16)
        mod3 = mod_all[l].reshape(batch, 3, d)
        proj, g, k, gsum, loga, lsum = _inproj(xf, mod3, norm_w[l][None], w_main, w_hf,
                                               lb_all[l][None], wga, aw, gla_alpha_b[l][None],
                                               seq=seq)
        oh, og = _scans(proj, k, g, gsum, loga, lsum, batch=batch, seq=seq)
        xf = _outproj(oh, og, proj, xf, mod3, hgrn_norm_w[l][None], gla_norm_w[l][None],
                      w_branch[l, 0].astype(BF16), w_branch[l, 1].astype(BF16),
                      w_out[l].astype(BF16), final_norm_w[None],
                      seq=seq, final=(l == depth - 1))
    return xf.reshape(batch, seq, d)
```

```python
import functools

import jax
import jax.numpy as jnp
from jax import lax
from jax.experimental import pallas as pl
from jax.experimental.pallas import tpu as pltpu

F32 = jnp.float32
BF16 = jnp.bfloat16

D_MODEL = 1024
DEPTH = 2
HGRN_HEADS = 8
GLA_HEADS = 4
HEAD_K = 128
HGRN_HEAD_V = 128
GLA_HEAD_V = 256
GLA_KEY = GLA_HEADS * HEAD_K
GLA_GATE_RANK = 16
GLA_GATE_TAU = 16.0
EPS = 1e-6
F_FLOOR = 1e-20

LANES = 128
SUBLANES = 8
CHUNK = 64
NBLK = CHUNK // SUBLANES
HALF = CHUNK // 2
HGRN_HEAD_GROUP = 2
GLA_HEAD_GROUP = 1
SAFE_DECAY = 70.0
OFFDIAG_LEVELS = (8, 16, 32)

COL_HF = 1024
COL_HI = 2048
COL_GA = 7168
PROJ_BLOCK = 1024
BLK_HQ, BLK_HI, BLK_HZ, BLK_GQK, BLK_GV, BLK_GZ, BLK_MG0, BLK_MG1 = range(8)
ACT_ID, ACT_SILU, ACT_SIGMOID = range(3)
BLOCK_ACT = (ACT_SILU, ACT_ID, ACT_SILU, ACT_ID, ACT_ID, ACT_SILU, ACT_SIGMOID, ACT_SIGMOID)
OUT_VMEM_LIMIT = 48 * 1024 * 1024
PROJ_VMEM_LIMIT = 56 * 1024 * 1024


def _silu(x):
    return x * jax.nn.sigmoid(x)


def _mod_kernel(c_ref, w_ref, b_ref, mod_ref):
    c = c_ref[...]
    c_act = _silu(c)
    mod_ref[0] = jnp.dot(c_act.astype(BF16), w_ref[0].astype(BF16),
                         preferred_element_type=F32) + b_ref[0]


def _lb_kernel(logit_ref, lb_ref):
    x = logit_ref[...]
    m = jnp.max(x, axis=0, keepdims=True)
    e = jnp.exp(x - m)
    p = e / jnp.sum(e, axis=0, keepdims=True)
    run = jnp.zeros_like(p[0:1])
    for l in range(x.shape[0]):
        run = run + p[l:l + 1]
        lb_ref[l:l + 1, :] = run - p[0:1]


def _prep(c, ada_w, ada_b, lb_logits):
    depth, d, d3 = ada_w.shape
    b = c.shape[0]
    nblk = d3 // d
    mod = pl.pallas_call(
        _mod_kernel,
        grid=(depth, nblk),
        in_specs=[
            pl.BlockSpec((b, d), lambda l, n: (0, 0)),
            pl.BlockSpec((1, d, d), lambda l, n: (l, 0, n)),
            pl.BlockSpec((1, b, d), lambda l, n: (l, 0, n)),
        ],
        out_specs=pl.BlockSpec((1, b, d), lambda l, n: (l, 0, n)),
        out_shape=jax.ShapeDtypeStruct((depth, b, d3), F32),
        name="adaln_mod",
    )(c, ada_w, jnp.broadcast_to(ada_b[:, None, :], (depth, b, d3)))
    lb = pl.pallas_call(
        _lb_kernel,
        out_shape=jax.ShapeDtypeStruct(lb_logits.shape, F32),
        name="hgrn_lower_bounds",
    )(lb_logits)
    return mod, lb


def _sigmoid(x):
    return 0.5 * jnp.tanh(0.5 * x) + 0.5


def _half_block_sums(g):
    rows, cols = g.shape
    return jnp.sum(g.reshape(rows // HALF, HALF, cols), axis=1)


def _step_acts(per_step):
    return [BLOCK_ACT[s:s + per_step] for s in range(0, len(BLOCK_ACT), per_step)]


def _step_has_acts(j, acts, per_step):
    hits = [j == s for s, a in enumerate(_step_acts(per_step)) if a == acts]
    return functools.reduce(jnp.logical_or, hits)


def _inproj_kernel(x_ref, mod_ref, nw_ref, w_ref, whf_ref, lb_ref, wga_ref, aw_ref, ab_ref,
                   out_ref, g_ref, k_ref, gsum_ref, loga_ref, lsum_ref, h_ref):
    j = pl.program_id(1)

    @pl.when(j == 0)
    def _():
        x = x_ref[...]
        ms = jnp.mean(x * x, axis=-1, keepdims=True)
        y = x * lax.rsqrt(ms + EPS) * nw_ref[...]
        shift = mod_ref[0, 0:1, :]
        scale = mod_ref[0, 1:2, :]
        hb = (y * (1.0 + scale) + shift).astype(BF16)
        h_ref[...] = hb
        lb = jnp.clip(lb_ref[...], 0.0, 1.0)
        hf = jnp.dot(hb, whf_ref[...], preferred_element_type=F32)
        f = lb + (1.0 - lb) * _sigmoid(hf)
        g = jnp.log(jnp.maximum(f, F_FLOOR))
        g_ref[...] = g
        k_ref[...] = (1.0 - f).astype(BF16)
        gsum_ref[...] = _half_block_sums(g)
        ga = jnp.dot(hb, wga_ref[...], preferred_element_type=F32)
        z = jnp.dot(ga.astype(BF16), aw_ref[...], preferred_element_type=F32) + ab_ref[...]
        log_sig = jnp.minimum(z, 0.0) - jnp.log(1.0 + jnp.exp(-jnp.abs(z)))
        loga = log_sig * (1.0 / GLA_GATE_TAU)
        loga_ref[...] = loga
        lsum_ref[...] = _half_block_sums(loga)

    per_step = out_ref.shape[1] // PROJ_BLOCK

    def project(acts):
        for s, act in enumerate(acts):
            cols = slice(s * PROJ_BLOCK, (s + 1) * PROJ_BLOCK)
            acc = jnp.dot(h_ref[...], w_ref[:, cols], preferred_element_type=F32)
            if act != ACT_ID:
                sig = _sigmoid(acc)
                acc = acc * sig if act == ACT_SILU else sig
            out_ref[:, cols] = acc.astype(BF16)

    for acts in sorted(set(_step_acts(per_step))):
        pl.when(_step_has_acts(j, acts, per_step))(functools.partial(project, acts))


def _inproj(xf, mod3, nw, w_main, w_hf, lb, wga, aw, ab, *, seq, tm=1024, tn=2 * PROJ_BLOCK):
    m, d = xf.shape
    n = w_main.shape[1]
    per_batch = seq // tm
    const = lambda shape: pl.BlockSpec(shape, lambda i, j: (0, 0))
    rows = lambda r, c: pl.BlockSpec((r, c), lambda i, j: (i, 0))
    return pl.pallas_call(
        _inproj_kernel,
        grid=(m // tm, n // tn),
        in_specs=[
            rows(tm, d),
            pl.BlockSpec((1, 3, d), lambda i, j: (i // per_batch, 0, 0)),
            const((1, d)),
            pl.BlockSpec((d, tn), lambda i, j: (0, j)),
            const((d, PROJ_BLOCK)),
            const((1, d)),
            const((d, LANES)),
            const((LANES, GLA_KEY)),
            const((1, GLA_KEY)),
        ],
        out_specs=[
            pl.BlockSpec((tm, tn), lambda i, j: (i, j)),
            rows(tm, d), rows(tm, d), rows(tm // HALF, d),
            rows(tm, GLA_KEY), rows(tm // HALF, GLA_KEY),
        ],
        out_shape=[
            jax.ShapeDtypeStruct((m, n), BF16),
            jax.ShapeDtypeStruct((m, d), F32),
            jax.ShapeDtypeStruct((m, d), BF16),
            jax.ShapeDtypeStruct((m // HALF, d), F32),
            jax.ShapeDtypeStruct((m, GLA_KEY), F32),
            jax.ShapeDtypeStruct((m // HALF, GLA_KEY), F32),
        ],
        scratch_shapes=[pltpu.VMEM((tm, d), BF16)],
        compiler_params=pltpu.CompilerParams(
            dimension_semantics=("parallel", "arbitrary"),
            vmem_limit_bytes=PROJ_VMEM_LIMIT),
        name="inproj",
    )(xf, mod3, nw, w_main, w_hf, lb, wga, aw, ab)


def _fast_masks(group):
    shape = (HALF, group * CHUNK)
    row = lax.broadcasted_iota(jnp.int32, shape, 0)
    col = lax.broadcasted_iota(jnp.int32, shape, 1) & (CHUNK - 1)
    top = col <= row
    left = col < HALF
    bot = (col >= HALF) & (col <= row + HALF)
    return top, left, bot


def _block_diag(blocks):
    n = len(blocks)
    return jnp.concatenate(
        [jnp.concatenate([b if j == i else jnp.zeros_like(b) for j in range(n)], axis=1)
         for i, b in enumerate(blocks)], axis=0)


def _block_cumsum8(g):
    g3 = g.reshape(NBLK, SUBLANES, g.shape[-1])
    sub = lax.broadcasted_iota(jnp.int32, g3.shape, 1)
    for sh in (1, 2, 4):
        g3 = g3 + jnp.where(sub >= sh, pltpu.roll(g3, sh, axis=1), 0.0)
    return g3


_NT = (((1,), (1,)), ((), ()))
_TN = (((0,), (0,)), ((), ()))


def _gla_chunk_fast(q, k, vb, g, st, masks, *, group):
    top_mask, left_mask, bot_mask = masks
    heads = st.shape[0]
    vdim = vb.shape[1] // heads
    ngroups = heads // group
    hcol = lambda h, w: slice(h * w, (h + 1) * w)
    gcol = lambda p, w: slice(p * group * w, (p + 1) * group * w)
    members = lambda p: range(p * group, (p + 1) * group)
    c8a = _block_cumsum8(g)
    c8 = [c8a[i] for i in range(NBLK)]
    t8 = [ci[SUBLANES - 1:SUBLANES, :] for ci in c8]
    per = HALF // SUBLANES
    c32, t32 = [], []
    for half in range(2):
        run = None
        for i in range(half * per, (half + 1) * per):
            c32.append(c8[i] if run is None else c8[i] + run)
            run = t8[i] if run is None else run + t8[i]
        t32.append(run)
    c32 = jnp.concatenate(c32, axis=0)
    e0 = jnp.exp(t32[0])
    e1 = jnp.exp(t32[1])
    e01 = e0 * e1
    q32 = q * jnp.exp(c32)
    kd = k * jnp.exp(-c32)
    q32b = q32.astype(BF16)
    q64_hi = (q32[HALF:] * e0).astype(BF16)
    lhs = jnp.concatenate([q32b, q64_hi], axis=0)
    kdb = kd.astype(BF16)
    q64 = jnp.concatenate([q32b[:HALF], q64_hi], axis=0)
    k64 = jnp.concatenate([kd[:HALF] * e01, kd[HALF:] * e1], axis=0).astype(BF16)

    batch_nt = (((2,), (2,)), ((0,), (0,)))
    batch_nn = (((2,), (1,)), ((0,), (0,)))
    batch_tn = (((1,), (1,)), ((0,), (0,)))
    lhs_b = jnp.stack([lhs[:, gcol(p, HEAD_K)] for p in range(ngroups)])
    kd_b = jnp.stack([_block_diag([kdb[:, hcol(h, HEAD_K)] for h in members(p)])
                      for p in range(ngroups)])
    sc = lax.dot_general(lhs_b, kd_b, batch_nt, preferred_element_type=F32)
    top = jnp.where(top_mask, sc[:, :HALF], 0.0)
    bot = jnp.where(left_mask, sc[:, CHUNK:], jnp.where(bot_mask, sc[:, HALF:CHUNK], 0.0))
    scores = jnp.concatenate([top, bot], axis=1).astype(BF16)
    vb_b = jnp.stack([_block_diag([vb[:, hcol(h, vdim)] for h in members(p)])
                      for p in range(ngroups)])
    st16 = st.astype(BF16)
    st_b = jnp.stack([_block_diag([st16[h] for h in members(p)]) for p in range(ngroups)])
    q64_b = jnp.stack([q64[:, gcol(p, HEAD_K)] for p in range(ngroups)])
    o = (lax.dot_general(scores, vb_b, batch_nn, preferred_element_type=F32)
         + lax.dot_general(q64_b, st_b, batch_nt, preferred_element_type=F32))
    o2d = jnp.concatenate([o[p] for p in range(ngroups)], axis=1)
    vb_h = jnp.stack([vb[:, hcol(h, vdim)] for h in range(heads)])
    k64_h = jnp.stack([k64[:, hcol(h, HEAD_K)] for h in range(heads)])
    upd = lax.dot_general(vb_h, k64_h, batch_tn, preferred_element_type=F32)
    e01_h = jnp.stack([e01[:, hcol(h, HEAD_K)] for h in range(heads)])
    return o2d, st * e01_h + upd


def _exact_masks():
    row = lax.broadcasted_iota(jnp.int32, (CHUNK, CHUNK), 0)
    col = lax.broadcasted_iota(jnp.int32, (CHUNK, CHUNK), 1)
    row_in = row & (SUBLANES - 1)
    row_base = row - row_in
    diag = [(col == row_base + s) & (row_in >= s) for s in range(SUBLANES)]
    levels = []
    for m in OFFDIAG_LEVELS:
        sh = m.bit_length() - 1
        tb = row >> sh
        sb = col >> sh
        levels.append(((tb & 1) == 1) & (sb == tb - 1))
    return diag, levels


def _gla_chunk_exact(q, k, vb, g, st, masks):
    diag_masks, level_masks = masks
    kdim = q.shape[-1]
    c8a = _block_cumsum8(g)

    c = {8: [c8a[i] for i in range(NBLK)]}
    t = {8: [ci[SUBLANES - 1:SUBLANES, :] for ci in c[8]]}
    m = 8
    while m < CHUNK:
        per = m // SUBLANES
        nxt = []
        for i in range(NBLK):
            mb = i // per
            nxt.append(c[m][i] + t[m][mb - 1] if mb % 2 == 1 else c[m][i])
        c[2 * m] = nxt
        t[2 * m] = [t[m][2 * j] + t[m][2 * j + 1] for j in range(len(t[m]) // 2)]
        m *= 2

    def cat(blocks):
        return jnp.concatenate(blocks, axis=0)

    def q_scaled(m):
        return (q * jnp.exp(cat(c[m]))).astype(BF16)

    def k_scaled(m):
        per = m // SUBLANES
        rest = [t[m][i // per] - c[m][i] for i in range(NBLK)]
        return (k * jnp.exp(cat(rest))).astype(BF16)

    q3 = q.reshape(NBLK, SUBLANES, kdim)
    k3 = k.reshape(NBLK, SUBLANES, kdim)
    scores = jnp.zeros((CHUNK, CHUNK), F32)
    for s in range(SUBLANES):
        e = jnp.exp(jnp.minimum(c8a - c8a[:, s:s + 1, :], 0.0))
        p = q3 * e * k3[:, s:s + 1, :]
        r = jnp.sum(p, axis=-1, keepdims=True).reshape(CHUNK, 1)
        scores = scores + jnp.where(diag_masks[s], r, 0.0)
    for m, mask in zip(OFFDIAG_LEVELS, level_masks):
        sc = lax.dot_general(q_scaled(m), k_scaled(m), _NT, preferred_element_type=F32)
        scores = scores + jnp.where(mask, sc, 0.0)

    o = jnp.dot(scores.astype(BF16), vb, preferred_element_type=F32)
    o = o + lax.dot_general(q_scaled(CHUNK), st.astype(BF16), _NT, preferred_element_type=F32)
    upd = lax.dot_general(vb, k_scaled(CHUNK), _TN, preferred_element_type=F32)
    st_new = st * jnp.exp(t[CHUNK][0]) + upd
    return o, st_new


def _scan_step(load_qkg, v_ref, gsum_ref, o_ref, st_ref, *, tb, heads, head_v, group):
    @pl.when(pl.program_id(1) == 0)
    def _():
        st_ref[...] = jnp.zeros_like(st_ref)

    safe = jnp.min(gsum_ref[...]) >= -SAFE_DECAY

    def head_chunk(chunk_fn, rows, h, masks):
        kc = slice(h * HEAD_K, (h + 1) * HEAD_K)
        vc = slice(h * head_v, (h + 1) * head_v)
        q, k, g = load_qkg(rows, kc)
        o, st_new = chunk_fn(q, k, v_ref[rows, vc], g, st_ref[h], masks)
        st_ref[h] = st_new
        o_ref[rows, vc] = o.astype(o_ref.dtype)

    @pl.when(safe)
    def _():
        masks = _fast_masks(group)
        for c in range(tb // CHUNK):
            rows = pl.ds(c * CHUNK, CHUNK)
            q, k, g = load_qkg(rows, slice(0, heads * HEAD_K))
            o, st_new = _gla_chunk_fast(q, k, v_ref[rows, :], g, st_ref[...], masks, group=group)
            st_ref[...] = st_new
            o_ref[rows, :] = o.astype(o_ref.dtype)

    @pl.when(jnp.logical_not(safe))
    def _():
        masks = _exact_masks()

        def body(ci, carry):
            rows = pl.ds(pl.multiple_of(ci * CHUNK, CHUNK), CHUNK)
            for h in range(heads):
                head_chunk(_gla_chunk_exact, rows, h, masks)
            return carry

        lax.fori_loop(0, tb // CHUNK, body, 0)


def _scan_kernel(hq_ref, hk_ref, hv_ref, hg_ref, hgsum_ref, gqk_ref, gv_ref, gg_ref, ggsum_ref,
                 oh_ref, og_ref, hst_ref, gst_ref, *, tb):
    def load_hgrn(rows, kc):
        return hq_ref[rows, kc].astype(F32), hk_ref[rows, kc].astype(F32), hg_ref[rows, kc]

    _scan_step(load_hgrn, hv_ref, hgsum_ref, oh_ref, hst_ref, tb=tb, heads=HGRN_HEADS,
               head_v=HGRN_HEAD_V, group=HGRN_HEAD_GROUP)

    scale = HEAD_K ** -0.5

    def load_gla(rows, kc):
        kc2 = slice(GLA_KEY + kc.start, GLA_KEY + kc.stop)
        return (gqk_ref[rows, kc].astype(F32) * scale, gqk_ref[rows, kc2].astype(F32),
                gg_ref[rows, kc])

    _scan_step(load_gla, gv_ref, ggsum_ref, og_ref, gst_ref, tb=tb, heads=GLA_HEADS,
               head_v=GLA_HEAD_V, group=GLA_HEAD_GROUP)


def _scans(proj, k, g, gsum, loga, lsum, *, batch, seq, tb=1024):
    m = proj.shape[0]
    d = D_MODEL
    nt = seq // tb
    blk = lambda r, c, col: pl.BlockSpec((r, c), lambda b, t: (b * nt + t, col))
    out = jax.ShapeDtypeStruct((m, d), BF16)
    return pl.pallas_call(
        functools.partial(_scan_kernel, tb=tb),
        grid=(batch, nt),
        in_specs=[blk(tb, PROJ_BLOCK, BLK_HQ), blk(tb, PROJ_BLOCK, 0), blk(tb, PROJ_BLOCK, BLK_HI),
                  blk(tb, PROJ_BLOCK, 0), blk(tb // HALF, PROJ_BLOCK, 0),
                  blk(tb, PROJ_BLOCK, BLK_GQK), blk(tb, PROJ_BLOCK, BLK_GV),
                  blk(tb, GLA_KEY, 0), blk(tb // HALF, GLA_KEY, 0)],
        out_specs=[blk(tb, d, 0), blk(tb, d, 0)],
        out_shape=[out, out],
        scratch_shapes=[pltpu.VMEM((HGRN_HEADS, HGRN_HEAD_V, HEAD_K), F32),
                        pltpu.VMEM((GLA_HEADS, GLA_HEAD_V, HEAD_K), F32)],
        compiler_params=pltpu.CompilerParams(dimension_semantics=("parallel", "arbitrary"),
                                             vmem_limit_bytes=PROJ_VMEM_LIMIT),
        name="scans",
    )(proj, k, proj, g, gsum, proj, proj, loga, lsum)


def _rms(x, w):
    return x * lax.rsqrt(jnp.mean(x * x, axis=-1, keepdims=True) + EPS) * w


def _out_kernel(oh_ref, og_ref, hz_ref, gz_ref, mg0_ref, mg1_ref, x_ref, mod_ref,
                hnw_ref, gnw_ref, wb0_ref, wb1_ref, wo_ref, fnw_ref, out_ref, *, final):
    f32 = lambda ref: ref[...].astype(F32)
    yh = _rms(f32(oh_ref), hnw_ref[...]) * f32(hz_ref)
    gnw = gnw_ref[...]
    og = jnp.concatenate(
        [_rms(og_ref[:, h * GLA_HEAD_V:(h + 1) * GLA_HEAD_V].astype(F32), gnw)
         for h in range(GLA_HEADS)], axis=-1)
    yg = og * f32(gz_ref)
    uh = jnp.dot(yh.astype(BF16), wb0_ref[...], preferred_element_type=F32)
    ug = jnp.dot(yg.astype(BF16), wb1_ref[...], preferred_element_type=F32)
    merged = f32(mg0_ref) * uh + f32(mg1_ref) * ug
    gate = mod_ref[0, 2:3, :]
    out = x_ref[...] + gate * jnp.dot(merged.astype(BF16), wo_ref[...],
                                      preferred_element_type=F32)
    if final:
        out = _rms(out, fnw_ref[...])
    out_ref[...] = out


def _outproj(oh, og, proj, xf, mod3, hnw, gnw, wb0, wb1, wo, fnw, *, seq, final, tm=512):
    m, d = xf.shape
    per_batch = seq // tm
    rows = lambda c: pl.BlockSpec((tm, PROJ_BLOCK), lambda i: (i, c))
    const = lambda shape: pl.BlockSpec(shape, lambda i: tuple(0 for _ in shape))
    return pl.pallas_call(
        functools.partial(_out_kernel, final=final),
        grid=(m // tm,),
        in_specs=[
            rows(0), rows(0), rows(BLK_HZ), rows(BLK_GZ), rows(BLK_MG0), rows(BLK_MG1), rows(0),
            pl.BlockSpec((1, 3, d), lambda i: (i // per_batch, 0, 0)),
            const((1, d)), const((1, GLA_HEAD_V)),
            const((d, d)), const((d, d)), const((d, d)), const((1, d)),
        ],
        out_specs=rows(0),
        out_shape=jax.ShapeDtypeStruct((m, d), F32),
        compiler_params=pltpu.CompilerParams(
            dimension_semantics=("parallel",), vmem_limit_bytes=OUT_VMEM_LIMIT),
        name="outproj",
    )(oh, og, proj, proj, proj, proj, xf, mod3, hnw, gnw, wb0, wb1, wo, fnw)


def kernel(x, c, ada_w, ada_b, norm_w, w_in, hgrn_lb_logits, hgrn_norm_w, gla_alpha_w,
           gla_alpha_b, gla_norm_w, w_branch, w_out, final_norm_w):
    batch, seq, d = x.shape
    depth = ada_w.shape[0]
    mod_all, lb_all = _prep(c, ada_w, ada_b, hgrn_lb_logits)
    xf = x.reshape(batch * seq, d)
    ga_hi = COL_GA + GLA_GATE_RANK
    for l in range(depth):
        w_main = jnp.concatenate([w_in[l][:, :COL_HF], w_in[l][:, COL_HI:COL_GA], w_in[l][:, ga_hi:]],
                                 axis=1).astype(BF16)
        w_hf = w_in[l][:, COL_HF:COL_HI].astype(BF16)
        wga = jnp.pad(w_in[l][:, COL_GA:ga_hi], ((0, 0), (0, LANES - GLA_GATE_RANK))).astype(BF16)
        aw = jnp.pad(gla_alpha_w[l], ((0, LANES - GLA_GATE_RANK), (0, 0))).astype(BF16)
        mod3 = mod_all[l].reshape(batch, 3, d)
        proj, g, k, gsum, loga, lsum = _inproj(xf, mod3, norm_w[l][None], w_main, w_hf,
                                               lb_all[l][None], wga, aw, gla_alpha_b[l][None],
                                               seq=seq)
        oh, og = _scans(proj, k, g, gsum, loga, lsum, batch=batch, seq=seq)
        xf = _outproj(oh, og, proj, xf, mod3, hgrn_norm_w[l][None], gla_norm_w[l][None],
                      w_branch[l, 0].astype(BF16), w_branch[l, 1].astype(BF16),
                      w_out[l].astype(BF16), final_norm_w[None],
                      seq=seq, final=(l == depth - 1))
    return xf.reshape(batch, seq, d)
```

```python
import functools

import jax
import jax.numpy as jnp
from jax import lax
from jax.experimental import pallas as pl
from jax.experimental.pallas import tpu as pltpu

F32 = jnp.float32
BF16 = jnp.bfloat16

D_MODEL = 1024
DEPTH = 2
HGRN_HEADS = 8
GLA_HEADS = 4
HEAD_K = 128
HGRN_HEAD_V = 128
GLA_HEAD_V = 256
GLA_KEY = GLA_HEADS * HEAD_K
GLA_GATE_RANK = 16
GLA_GATE_TAU = 16.0
EPS = 1e-6
F_FLOOR = 1e-20

LANES = 128
SUBLANES = 8
CHUNK = 64
NBLK = CHUNK // SUBLANES
HALF = CHUNK // 2
HGRN_HEAD_GROUP = 2
GLA_HEAD_GROUP = 1
SAFE_DECAY = 70.0
OFFDIAG_LEVELS = (8, 16, 32)

COL_HF = 1024
COL_HI = 2048
COL_GA = 7168
PROJ_BLOCK = 1024
BLK_HQ, BLK_HI, BLK_HZ, BLK_GQK, BLK_GV, BLK_GZ, BLK_MG0, BLK_MG1 = range(8)
ACT_ID, ACT_SILU, ACT_SIGMOID = range(3)
BLOCK_ACT = (ACT_SILU, ACT_ID, ACT_SILU, ACT_ID, ACT_ID, ACT_SILU, ACT_SIGMOID, ACT_SIGMOID)
PROJ_VMEM_LIMIT = 56 * 1024 * 1024


def _silu(x):
    return x * jax.nn.sigmoid(x)


def _mod_kernel(c_ref, w_ref, b_ref, mod_ref):
    c = c_ref[...]
    c_act = _silu(c)
    mod_ref[0] = jnp.dot(c_act.astype(BF16), w_ref[0].astype(BF16),
                         preferred_element_type=F32) + b_ref[0]


def _lb_kernel(logit_ref, lb_ref):
    x = logit_ref[...]
    m = jnp.max(x, axis=0, keepdims=True)
    e = jnp.exp(x - m)
    p = e / jnp.sum(e, axis=0, keepdims=True)
    run = jnp.zeros_like(p[0:1])
    for l in range(x.shape[0]):
        run = run + p[l:l + 1]
        lb_ref[l:l + 1, :] = run - p[0:1]


def _prep(c, ada_w, ada_b, lb_logits):
    depth, d, d3 = ada_w.shape
    b = c.shape[0]
    nblk = d3 // d
    mod = pl.pallas_call(
        _mod_kernel,
        grid=(depth, nblk),
        in_specs=[
            pl.BlockSpec((b, d), lambda l, n: (0, 0)),
            pl.BlockSpec((1, d, d), lambda l, n: (l, 0, n)),
            pl.BlockSpec((1, b, d), lambda l, n: (l, 0, n)),
        ],
        out_specs=pl.BlockSpec((1, b, d), lambda l, n: (l, 0, n)),
        out_shape=jax.ShapeDtypeStruct((depth, b, d3), F32),
        name="adaln_mod",
    )(c, ada_w, jnp.broadcast_to(ada_b[:, None, :], (depth, b, d3)))
    lb = pl.pallas_call(
        _lb_kernel,
        out_shape=jax.ShapeDtypeStruct(lb_logits.shape, F32),
        name="hgrn_lower_bounds",
    )(lb_logits)
    return mod, lb


def _sigmoid(x):
    return 0.5 * jnp.tanh(0.5 * x) + 0.5


def _half_block_sums(g):
    rows, cols = g.shape
    return jnp.sum(g.reshape(rows // HALF, HALF, cols), axis=1)


def _step_acts(per_step):
    return [BLOCK_ACT[s:s + per_step] for s in range(0, len(BLOCK_ACT), per_step)]


def _step_has_acts(j, acts, per_step):
    hits = [j == s for s, a in enumerate(_step_acts(per_step)) if a == acts]
    return functools.reduce(jnp.logical_or, hits)


def _inproj_kernel(x_ref, mod_ref, nw_ref, w_ref, whf_ref, lb_ref, wga_ref, aw_ref, ab_ref,
                   out_ref, g_ref, k_ref, gsum_ref, loga_ref, lsum_ref, h_ref):
    j = pl.program_id(1)

    @pl.when(j == 0)
    def _():
        x = x_ref[...]
        ms = jnp.mean(x * x, axis=-1, keepdims=True)
        y = x * lax.rsqrt(ms + EPS) * nw_ref[...]
        shift = mod_ref[0, 0:1, :]
        scale = mod_ref[0, 1:2, :]
        hb = (y * (1.0 + scale) + shift).astype(BF16)
        h_ref[...] = hb
        lb = jnp.clip(lb_ref[...], 0.0, 1.0)
        hf = jnp.dot(hb, whf_ref[...], preferred_element_type=F32)
        f = lb + (1.0 - lb) * _sigmoid(hf)
        g = jnp.log(jnp.maximum(f, F_FLOOR))
        g_ref[...] = g
        k_ref[...] = (1.0 - f).astype(BF16)
        gsum_ref[...] = _half_block_sums(g)
        ga = jnp.dot(hb, wga_ref[...], preferred_element_type=F32)
        z = jnp.dot(ga.astype(BF16), aw_ref[...], preferred_element_type=F32) + ab_ref[...]
        log_sig = jnp.minimum(z, 0.0) - jnp.log(1.0 + jnp.exp(-jnp.abs(z)))
        loga = log_sig * (1.0 / GLA_GATE_TAU)
        loga_ref[...] = loga
        lsum_ref[...] = _half_block_sums(loga)

    per_step = out_ref.shape[1] // PROJ_BLOCK

    def project(acts):
        for s, act in enumerate(acts):
            cols = slice(s * PROJ_BLOCK, (s + 1) * PROJ_BLOCK)
            acc = jnp.dot(h_ref[...], w_ref[:, cols], preferred_element_type=F32)
            if act != ACT_ID:
                sig = _sigmoid(acc)
                acc = acc * sig if act == ACT_SILU else sig
            out_ref[:, cols] = acc.astype(BF16)

    for acts in sorted(set(_step_acts(per_step))):
        pl.when(_step_has_acts(j, acts, per_step))(functools.partial(project, acts))


def _inproj(xf, mod3, nw, w_main, w_hf, lb, wga, aw, ab, *, seq, tm=1024, tn=2 * PROJ_BLOCK):
    m, d = xf.shape
    n = w_main.shape[1]
    per_batch = seq // tm
    const = lambda shape: pl.BlockSpec(shape, lambda i, j: (0, 0))
    rows = lambda r, c: pl.BlockSpec((r, c), lambda i, j: (i, 0))
    return pl.pallas_call(
        _inproj_kernel,
        grid=(m // tm, n // tn),
        in_specs=[
            rows(tm, d),
            pl.BlockSpec((1, 3, d), lambda i, j: (i // per_batch, 0, 0)),
            const((1, d)),
            pl.BlockSpec((d, tn), lambda i, j: (0, j)),
            const((d, PROJ_BLOCK)),
            const((1, d)),
            const((d, LANES)),
            const((LANES, GLA_KEY)),
            const((1, GLA_KEY)),
        ],
        out_specs=[
            pl.BlockSpec((tm, tn), lambda i, j: (i, j)),
            rows(tm, d), rows(tm, d), rows(tm // HALF, d),
            rows(tm, GLA_KEY), rows(tm // HALF, GLA_KEY),
        ],
        out_shape=[
            jax.ShapeDtypeStruct((m, n), BF16),
            jax.ShapeDtypeStruct((m, d), F32),
            jax.ShapeDtypeStruct((m, d), BF16),
            jax.ShapeDtypeStruct((m // HALF, d), F32),
            jax.ShapeDtypeStruct((m, GLA_KEY), F32),
            jax.ShapeDtypeStruct((m // HALF, GLA_KEY), F32),
        ],
        scratch_shapes=[pltpu.VMEM((tm, d), BF16)],
        compiler_params=pltpu.CompilerParams(
            dimension_semantics=("parallel", "arbitrary"),
            vmem_limit_bytes=PROJ_VMEM_LIMIT),
        name="inproj",
    )(xf, mod3, nw, w_main, w_hf, lb, wga, aw, ab)


def _fast_masks(group):
    shape = (HALF, group * CHUNK)
    row = lax.broadcasted_iota(jnp.int32, shape, 0)
    col = lax.broadcasted_iota(jnp.int32, shape, 1) & (CHUNK - 1)
    top = col <= row
    left = col < HALF
    bot = (col >= HALF) & (col <= row + HALF)
    return top, left, bot


def _block_diag(blocks):
    n = len(blocks)
    return jnp.concatenate(
        [jnp.concatenate([b if j == i else jnp.zeros_like(b) for j in range(n)], axis=1)
         for i, b in enumerate(blocks)], axis=0)


def _block_cumsum8(g):
    g3 = g.reshape(NBLK, SUBLANES, g.shape[-1])
    sub = lax.broadcasted_iota(jnp.int32, g3.shape, 1)
    for sh in (1, 2, 4):
        g3 = g3 + jnp.where(sub >= sh, pltpu.roll(g3, sh, axis=1), 0.0)
    return g3


_NT = (((1,), (1,)), ((), ()))
_TN = (((0,), (0,)), ((), ()))


def _gla_chunk_fast(q, k, vb, g, st, masks, *, group):
    top_mask, left_mask, bot_mask = masks
    heads = st.shape[0]
    vdim = vb.shape[1] // heads
    ngroups = heads // group
    hcol = lambda h, w: slice(h * w, (h + 1) * w)
    gcol = lambda p, w: slice(p * group * w, (p + 1) * group * w)
    members = lambda p: range(p * group, (p + 1) * group)
    c8a = _block_cumsum8(g)
    c8 = [c8a[i] for i in range(NBLK)]
    t8 = [ci[SUBLANES - 1:SUBLANES, :] for ci in c8]
    per = HALF // SUBLANES
    c32, t32 = [], []
    for half in range(2):
        run = None
        for i in range(half * per, (half + 1) * per):
            c32.append(c8[i] if run is None else c8[i] + run)
            run = t8[i] if run is None else run + t8[i]
        t32.append(run)
    c32 = jnp.concatenate(c32, axis=0)
    e0 = jnp.exp(t32[0])
    e1 = jnp.exp(t32[1])
    e01 = e0 * e1
    q32 = q * jnp.exp(c32)
    kd = k * jnp.exp(-c32)
    q32b = q32.astype(BF16)
    q64_hi = (q32[HALF:] * e0).astype(BF16)
    lhs = jnp.concatenate([q32b, q64_hi], axis=0)
    kdb = kd.astype(BF16)
    q64 = jnp.concatenate([q32b[:HALF], q64_hi], axis=0)
    k64 = jnp.concatenate([kd[:HALF] * e01, kd[HALF:] * e1], axis=0).astype(BF16)

    batch_nt = (((2,), (2,)), ((0,), (0,)))
    batch_nn = (((2,), (1,)), ((0,), (0,)))
    batch_tn = (((1,), (1,)), ((0,), (0,)))
    lhs_b = jnp.stack([lhs[:, gcol(p, HEAD_K)] for p in range(ngroups)])
    kd_b = jnp.stack([_block_diag([kdb[:, hcol(h, HEAD_K)] for h in members(p)])
                      for p in range(ngroups)])
    sc = lax.dot_general(lhs_b, kd_b, batch_nt, preferred_element_type=F32)
    top = jnp.where(top_mask, sc[:, :HALF], 0.0)
    bot = jnp.where(left_mask, sc[:, CHUNK:], jnp.where(bot_mask, sc[:, HALF:CHUNK], 0.0))
    scores = jnp.concatenate([top, bot], axis=1).astype(BF16)
    vb_b = jnp.stack([_block_diag([vb[:, hcol(h, vdim)] for h in members(p)])
                      for p in range(ngroups)])
    st16 = st.astype(BF16)
    st_b = jnp.stack([_block_diag([st16[h] for h in members(p)]) for p in range(ngroups)])
    q64_b = jnp.stack([q64[:, gcol(p, HEAD_K)] for p in range(ngroups)])
    o = (lax.dot_general(scores, vb_b, batch_nn, preferred_element_type=F32)
         + lax.dot_general(q64_b, st_b, batch_nt, preferred_element_type=F32))
    o2d = jnp.concatenate([o[p] for p in range(ngroups)], axis=1)
    vb_h = jnp.stack([vb[:, hcol(h, vdim)] for h in range(heads)])
    k64_h = jnp.stack([k64[:, hcol(h, HEAD_K)] for h in range(heads)])
    upd = lax.dot_general(vb_h, k64_h, batch_tn, preferred_element_type=F32)
    e01_h = jnp.stack([e01[:, hcol(h, HEAD_K)] for h in range(heads)])
    return o2d, st * e01_h + upd


def _exact_masks():
    row = lax.broadcasted_iota(jnp.int32, (CHUNK, CHUNK), 0)
    col = lax.broadcasted_iota(jnp.int32, (CHUNK, CHUNK), 1)
    row_in = row & (SUBLANES - 1)
    row_base = row - row_in
    diag = [(col == row_base + s) & (row_in >= s) for s in range(SUBLANES)]
    levels = []
    for m in OFFDIAG_LEVELS:
        sh = m.bit_length() - 1
        tb = row >> sh
        sb = col >> sh
        levels.append(((tb & 1) == 1) & (sb == tb - 1))
    return diag, levels


def _gla_chunk_exact(q, k, vb, g, st, masks):
    diag_masks, level_masks = masks
    kdim = q.shape[-1]
    c8a = _block_cumsum8(g)

    c = {8: [c8a[i] for i in range(NBLK)]}
    t = {8: [ci[SUBLANES - 1:SUBLANES, :] for ci in c[8]]}
    m = 8
    while m < CHUNK:
        per = m // SUBLANES
        nxt = []
        for i in range(NBLK):
            mb = i // per
            nxt.append(c[m][i] + t[m][mb - 1] if mb % 2 == 1 else c[m][i])
        c[2 * m] = nxt
        t[2 * m] = [t[m][2 * j] + t[m][2 * j + 1] for j in range(len(t[m]) // 2)]
        m *= 2

    def cat(blocks):
        return jnp.concatenate(blocks, axis=0)

    def q_scaled(m):
        return (q * jnp.exp(cat(c[m]))).astype(BF16)

    def k_scaled(m):
        per = m // SUBLANES
        rest = [t[m][i // per] - c[m][i] for i in range(NBLK)]
        return (k * jnp.exp(cat(rest))).astype(BF16)

    q3 = q.reshape(NBLK, SUBLANES, kdim)
    k3 = k.reshape(NBLK, SUBLANES, kdim)
    scores = jnp.zeros((CHUNK, CHUNK), F32)
    for s in range(SUBLANES):
        e = jnp.exp(jnp.minimum(c8a - c8a[:, s:s + 1, :], 0.0))
        p = q3 * e * k3[:, s:s + 1, :]
        r = jnp.sum(p, axis=-1, keepdims=True).reshape(CHUNK, 1)
        scores = scores + jnp.where(diag_masks[s], r, 0.0)
    for m, mask in zip(OFFDIAG_LEVELS, level_masks):
        sc = lax.dot_general(q_scaled(m), k_scaled(m), _NT, preferred_element_type=F32)
        scores = scores + jnp.where(mask, sc, 0.0)

    o = jnp.dot(scores.astype(BF16), vb, preferred_element_type=F32)
    o = o + lax.dot_general(q_scaled(CHUNK), st.astype(BF16), _NT, preferred_element_type=F32)
    upd = lax.dot_general(vb, k_scaled(CHUNK), _TN, preferred_element_type=F32)
    st_new = st * jnp.exp(t[CHUNK][0]) + upd
    return o, st_new


def _scan_step(load_qkg, v_ref, gsum_ref, o_ref, st_ref, *, tb, heads, head_v, group):
    @pl.when(pl.program_id(1) == 0)
    def _():
        st_ref[...] = jnp.zeros_like(st_ref)

    safe = jnp.min(gsum_ref[...]) >= -SAFE_DECAY

    def head_chunk(chunk_fn, rows, h, masks):
        kc = slice(h * HEAD_K, (h + 1) * HEAD_K)
        vc = slice(h * head_v, (h + 1) * head_v)
        q, k, g = load_qkg(rows, kc)
        o, st_new = chunk_fn(q, k, v_ref[rows, vc], g, st_ref[h], masks)
        st_ref[h] = st_new
        o_ref[rows, vc] = o.astype(o_ref.dtype)

    @pl.when(safe)
    def _():
        masks = _fast_masks(group)
        for c in range(tb // CHUNK):
            rows = pl.ds(c * CHUNK, CHUNK)
            q, k, g = load_qkg(rows, slice(0, heads * HEAD_K))
            o, st_new = _gla_chunk_fast(q, k, v_ref[rows, :], g, st_ref[...], masks, group=group)
            st_ref[...] = st_new
            o_ref[rows, :] = o.astype(o_ref.dtype)

    @pl.when(jnp.logical_not(safe))
    def _():
        masks = _exact_masks()

        def body(ci, carry):
            rows = pl.ds(pl.multiple_of(ci * CHUNK, CHUNK), CHUNK)
            for h in range(heads):
                head_chunk(_gla_chunk_exact, rows, h, masks)
            return carry

        lax.fori_loop(0, tb // CHUNK, body, 0)


def _scan_kernel(hq_ref, hk_ref, hv_ref, hg_ref, hgsum_ref, gqk_ref, gv_ref, gg_ref, ggsum_ref,
                 oh_ref, og_ref, hst_ref, gst_ref, *, tb):
    def load_hgrn(rows, kc):
        return hq_ref[rows, kc].astype(F32), hk_ref[rows, kc].astype(F32), hg_ref[rows, kc]

    _scan_step(load_hgrn, hv_ref, hgsum_ref, oh_ref, hst_ref, tb=tb, heads=HGRN_HEADS,
               head_v=HGRN_HEAD_V, group=HGRN_HEAD_GROUP)

    scale = HEAD_K ** -0.5

    def load_gla(rows, kc):
        kc2 = slice(GLA_KEY + kc.start, GLA_KEY + kc.stop)
        return (gqk_ref[rows, kc].astype(F32) * scale, gqk_ref[rows, kc2].astype(F32),
                gg_ref[rows, kc])

    _scan_step(load_gla, gv_ref, ggsum_ref, og_ref, gst_ref, tb=tb, heads=GLA_HEADS,
               head_v=GLA_HEAD_V, group=GLA_HEAD_GROUP)


def _scans(proj, k, g, gsum, loga, lsum, *, batch, seq, tb=1024):
    m = proj.shape[0]
    d = D_MODEL
    nt = seq // tb
    blk = lambda r, c, col: pl.BlockSpec((r, c), lambda b, t: (b * nt + t, col))
    out = jax.ShapeDtypeStruct((m, d), BF16)
    return pl.pallas_call(
        functools.partial(_scan_kernel, tb=tb),
        grid=(batch, nt),
        in_specs=[blk(tb, PROJ_BLOCK, BLK_HQ), blk(tb, PROJ_BLOCK, 0), blk(tb, PROJ_BLOCK, BLK_HI),
                  blk(tb, PROJ_BLOCK, 0), blk(tb // HALF, PROJ_BLOCK, 0),
                  blk(tb, PROJ_BLOCK, BLK_GQK), blk(tb, PROJ_BLOCK, BLK_GV),
                  blk(tb, GLA_KEY, 0), blk(tb // HALF, GLA_KEY, 0)],
        out_specs=[blk(tb, d, 0), blk(tb, d, 0)],
        out_shape=[out, out],
        scratch_shapes=[pltpu.VMEM((HGRN_HEADS, HGRN_HEAD_V, HEAD_K), F32),
                        pltpu.VMEM((GLA_HEADS, GLA_HEAD_V, HEAD_K), F32)],
        compiler_params=pltpu.CompilerParams(dimension_semantics=("parallel", "arbitrary"),
                                             vmem_limit_bytes=PROJ_VMEM_LIMIT),
        name="scans",
    )(proj, k, proj, g, gsum, proj, proj, loga, lsum)


def _rms(x, w):
    return x * lax.rsqrt(jnp.mean(x * x, axis=-1, keepdims=True) + EPS) * w


def _out_kernel(oh_ref, og_ref, hz_ref, gz_ref, mg0_ref, mg1_ref, x_ref, mod_ref,
                hnw_ref, gnw_ref, wb0_ref, wb1_ref, wo_ref, fnw_ref, out_ref, *, final):
    f32 = lambda ref: ref[...].astype(F32)
    yh = _rms(f32(oh_ref), hnw_ref[...]) * f32(hz_ref)
    gnw = gnw_ref[...]
    og = jnp.concatenate(
        [_rms(og_ref[:, h * GLA_HEAD_V:(h + 1) * GLA_HEAD_V].astype(F32), gnw)
         for h in range(GLA_HEADS)], axis=-1)
    yg = og * f32(gz_ref)
    uh = jnp.dot(yh.astype(BF16), wb0_ref[...], preferred_element_type=F32)
    ug = jnp.dot(yg.astype(BF16), wb1_ref[...], preferred_element_type=F32)
    merged = f32(mg0_ref) * uh + f32(mg1_ref) * ug
    gate = mod_ref[0, 2:3, :]
    out = x_ref[...] + gate * jnp.dot(merged.astype(BF16), wo_ref[...],
                                      preferred_element_type=F32)
    if final:
        out = _rms(out, fnw_ref[...])
    out_ref[...] = out


def _outproj(oh, og, proj, xf, mod3, hnw, gnw, wb0, wb1, wo, fnw, *, seq, final, tm=1024):
    m, d = xf.shape
    per_batch = seq // tm
    rows = lambda c: pl.BlockSpec((tm, PROJ_BLOCK), lambda i: (i, c))
    const = lambda shape: pl.BlockSpec(shape, lambda i: tuple(0 for _ in shape),
                                       pipeline_mode=pl.Buffered(1))
    return pl.pallas_call(
        functools.partial(_out_kernel, final=final),
        grid=(m // tm,),
        in_specs=[
            rows(0), rows(0), rows(BLK_HZ), rows(BLK_GZ), rows(BLK_MG0), rows(BLK_MG1), rows(0),
            pl.BlockSpec((1, 3, d), lambda i: (i // per_batch, 0, 0)),
            const((1, d)), const((1, GLA_HEAD_V)),
            const((d, d)), const((d, d)), const((d, d)), const((1, d)),
        ],
        out_specs=rows(0),
        out_shape=jax.ShapeDtypeStruct((m, d), F32),
        compiler_params=pltpu.CompilerParams(
            dimension_semantics=("parallel",), vmem_limit_bytes=PROJ_VMEM_LIMIT),
        name="outproj",
    )(oh, og, proj, proj, proj, proj, xf, mod3, hnw, gnw, wb0, wb1, wo, fnw)


def kernel(x, c, ada_w, ada_b, norm_w, w_in, hgrn_lb_logits, hgrn_norm_w, gla_alpha_w,
           gla_alpha_b, gla_norm_w, w_branch, w_out, final_norm_w):
    batch, seq, d = x.shape
    depth = ada_w.shape[0]
    mod_all, lb_all = _prep(c, ada_w, ada_b, hgrn_lb_logits)
    xf = x.reshape(batch * seq, d)
    ga_hi = COL_GA + GLA_GATE_RANK
    for l in range(depth):
        w_main = jnp.concatenate([w_in[l][:, :COL_HF], w_in[l][:, COL_HI:COL_GA], w_in[l][:, ga_hi:]],
                                 axis=1).astype(BF16)
        w_hf = w_in[l][:, COL_HF:COL_HI].astype(BF16)
        wga = jnp.pad(w_in[l][:, COL_GA:ga_hi], ((0, 0), (0, LANES - GLA_GATE_RANK))).astype(BF16)
        aw = jnp.pad(gla_alpha_w[l], ((0, LANES - GLA_GATE_RANK), (0, 0))).astype(BF16)
        mod3 = mod_all[l].reshape(batch, 3, d)
        proj, g, k, gsum, loga, lsum = _inproj(xf, mod3, norm_w[l][None], w_main, w_hf,
                                               lb_all[l][None], wga, aw, gla_alpha_b[l][None],
                                               seq=seq)
        oh, og = _scans(proj, k, g, gsum, loga, lsum, batch=batch, seq=seq)
        xf = _outproj(oh, og, proj, xf, mod3, hgrn_norm_w[l][None], gla_norm_w[l][None],
                      w_branch[l, 0].astype(BF16), w_branch[l, 1].astype(BF16),
                      w_out[l].astype(BF16), final_norm_w[None],
                      seq=seq, final=(l == depth - 1))
    return xf.reshape(batch, seq, d)
```
